```python
import jax, jax.numpy as jnp
from jax import lax
import numpy as np

D_MODEL = 1024
BATCH = 8
SEQ = 8192
DEPTH = 2
DEC_BATCH = 4
DEC_SEQ = 4096
PAST_LEN = 128

HEAD_DIM = 64
A_Q_HEADS = 8
A_KV_HEADS = 2
WINDOW = 128
A_BLOCK = 128
B_HEADS = 4
GRID_W = 64
NB_ROWS_MAX = 8
NB_COLS = 16
M_HEADS = 4
N_MEM = 256
D_FF = 4 * D_MODEL
EPS = 1e-6

A_Q = A_Q_HEADS * HEAD_DIM
A_KV = A_KV_HEADS * HEAD_DIM
B_W = B_HEADS * HEAD_DIM
M_W = M_HEADS * HEAD_DIM
MIX_W = A_Q + B_W + M_W
IN_W = A_Q + 2 * A_KV + 3 * B_W + M_W
SPLITS = [A_Q, A_Q + A_KV, A_Q + 2 * A_KV, A_Q + 2 * A_KV + B_W,
          A_Q + 2 * A_KV + 2 * B_W, A_Q + 2 * A_KV + 3 * B_W]

kernel_name = "hybrid_window_natten_memory_encoder"


def _rmsnorm(x, g):
    xf = x.astype(jnp.float32)
    y = xf * lax.rsqrt(jnp.mean(jnp.square(xf), axis=-1, keepdims=True) + EPS)
    return (y * g.astype(jnp.float32)).astype(x.dtype)


def _alibi_slopes(n):
    return jnp.exp2(-8.0 * jnp.arange(1, n + 1, dtype=jnp.float32) / n)


def _window_gqa(q, k, v, sink):
    B, S = q.shape[0], q.shape[1]
    nb = S // A_BLOCK
    G = A_Q_HEADS // A_KV_HEADS
    qb = q.reshape(B, nb, A_BLOCK, A_KV_HEADS, G, HEAD_DIM)
    pad = ((0, 0), (A_BLOCK, A_BLOCK), (0, 0), (0, 0))

    def band(t):
        tb = jnp.pad(t, pad).reshape(B, nb + 2, A_BLOCK, A_KV_HEADS, HEAD_DIM)
        return jnp.concatenate([tb[:, :-2], tb[:, 1:-1], tb[:, 2:]], axis=2)

    kb, vb = band(k), band(v)
    s = jnp.einsum('bnqhgd,bnshd->bnhgqs', qb, kb,
                   preferred_element_type=jnp.float32) * (HEAD_DIM ** -0.5)
    qi = jnp.arange(A_BLOCK)
    si = jnp.arange(3 * A_BLOCK)
    dist = jnp.abs(si[None, :] - A_BLOCK - qi[:, None]).astype(jnp.float32)
    kpos = jnp.arange(nb)[:, None] * A_BLOCK - A_BLOCK + si[None, :]
    valid = (dist <= WINDOW)[None] & ((kpos >= 0) & (kpos < S))[:, None, :]
    slopes = _alibi_slopes(A_Q_HEADS).reshape(A_KV_HEADS, G)
    s = s - slopes[:, :, None, None] * dist
    s = jnp.where(valid[None, :, None, None], s, -jnp.inf)
    sk = sink.astype(jnp.float32).reshape(A_KV_HEADS, G)[None, None, :, :, None, None]
    m = jnp.maximum(jnp.max(s, axis=-1, keepdims=True), sk)
    p = jnp.exp(s - m)
    p = p / (jnp.sum(p, axis=-1, keepdims=True) + jnp.exp(sk - m))
    o = jnp.einsum('bnhgqs,bnshd->bnqhgd', p.astype(v.dtype), vb)
    return o.reshape(B, S, A_Q)


def _neighborhood_attn(q, k, v, rpb):
    B, S = q.shape[0], q.shape[1]
    rows = S // GRID_W
    kh = min(NB_ROWS_MAX, rows)
    r = jnp.arange(rows)
    ridx = jnp.clip(r - kh // 2, 0, rows - kh)[:, None] + jnp.arange(kh)[None, :]
    c = jnp.arange(GRID_W)
    cstart = jnp.clip(c - NB_COLS // 2, 0, GRID_W - NB_COLS)
    cvalid = (c[None, :] >= cstart[:, None]) & (c[None, :] < cstart[:, None] + NB_COLS)
    qg = q.reshape(B, rows, GRID_W, B_HEADS, HEAD_DIM)
    kg = k.reshape(B, rows, GRID_W, B_HEADS, HEAD_DIM)[:, ridx]
    vg = v.reshape(B, rows, GRID_W, B_HEADS, HEAD_DIM)[:, ridx]
    s = jnp.einsum('brqhd,brkwhd->brhqkw', qg, kg,
                   preferred_element_type=jnp.float32) * (HEAD_DIM ** -0.5)
    dr = ridx - r[:, None] + (NB_ROWS_MAX - 1)
    dc = jnp.clip(c[None, :] - c[:, None] + (NB_COLS - 1), 0, 2 * NB_COLS - 2)
    bias = rpb.astype(jnp.float32)[:, dr[:, None, :, None], dc[None, :, None, :]]
    s = s + jnp.transpose(bias, (1, 0, 2, 3, 4))[None]
    s = jnp.where(cvalid[:, None, :][None, None, None], s, -jnp.inf)
    sh = s.shape
    p = jax.nn.softmax(s.reshape(sh[:4] + (kh * GRID_W,)), axis=-1).reshape(sh)
    o = jnp.einsum('brhqkw,brkwhd->brqhd', p.astype(v.dtype), vg)
    return o.reshape(B, S, B_W)


def _memory_attn(q, k, v):
    B, S = q.shape[0], q.shape[1]
    s = jnp.einsum('bshd,bmhd->bhsm', q, k, preferred_element_type=jnp.float32) * (HEAD_DIM ** -0.5)
    p = jax.nn.softmax(s, axis=-1)
    o = jnp.einsum('bhsm,bmhd->bshd', p.astype(v.dtype), v)
    return o.reshape(B, S, M_W)


def _trunk(x, mem, g_mix, w_in, qk_gain, sink, rpb, o_gain, w_out, g_mem, w_mem_kv, g_ff, w_ff1, w_ff2):
    B, S, _ = x.shape
    for l in range(DEPTH):
        h = _rmsnorm(x, g_mix[l])
        proj = h @ w_in[l]
        qa, ka, va, qb, kb, vb, qm = jnp.split(proj, SPLITS, axis=-1)
        qa = _rmsnorm(qa.reshape(B, S, A_Q_HEADS, HEAD_DIM), qk_gain[l, 0])
        ka = _rmsnorm(ka.reshape(B, S, A_KV_HEADS, HEAD_DIM), qk_gain[l, 1])
        va = va.reshape(B, S, A_KV_HEADS, HEAD_DIM)
        oa = _window_gqa(qa, ka, va, sink[l])
        qb = _rmsnorm(qb.reshape(B, S, B_HEADS, HEAD_DIM), qk_gain[l, 2])
        kb = _rmsnorm(kb.reshape(B, S, B_HEADS, HEAD_DIM), qk_gain[l, 3])
        vb = vb.reshape(B, S, B_HEADS, HEAD_DIM)
        ob = _neighborhood_attn(qb, kb, vb, rpb[l])
        mkv = _rmsnorm(mem, g_mem[l]) @ w_mem_kv[l]
        km, vm = jnp.split(mkv, 2, axis=-1)
        km = _rmsnorm(km.reshape(B, N_MEM, M_HEADS, HEAD_DIM), qk_gain[l, 5])
        vm = vm.reshape(B, N_MEM, M_HEADS, HEAD_DIM)
        qm = _rmsnorm(qm.reshape(B, S, M_HEADS, HEAD_DIM), qk_gain[l, 4])
        om = _memory_attn(qm, km, vm)
        og = o_gain[l]
        o = jnp.concatenate([_rmsnorm(oa, og[:A_Q]),
                             _rmsnorm(ob, og[A_Q:A_Q + B_W]),
                             _rmsnorm(om, og[A_Q + B_W:])], axis=-1)
        x = x + o @ w_out[l]
        f = _rmsnorm(x, g_ff[l]) @ w_ff1[l]
        x = x + jnp.square(jax.nn.relu(f)) @ w_ff2[l]
    return x


def setup_inputs(seed: int = 0) -> dict:
    key = jax.random.key(seed)
    ks = jax.random.split(key, 16)
    f32 = jnp.float32
    nrm = lambda k, shape: jax.random.normal(k, shape, dtype=f32)
    return {
        "x_prompt": nrm(ks[0], (BATCH, SEQ, D_MODEL)),
        "x_sample": nrm(ks[1], (DEC_BATCH, DEC_SEQ, D_MODEL)),
        "mem_prompt": nrm(ks[2], (BATCH, N_MEM, D_MODEL)),
        "mem_sample": nrm(ks[3], (DEC_BATCH, N_MEM, D_MODEL)),
        "g_mix": 1.0 + 0.02 * nrm(ks[4], (DEPTH, D_MODEL)),
        "w_in": nrm(ks[5], (DEPTH, D_MODEL, IN_W)) * D_MODEL ** -0.5,
        "qk_gain": 1.0 + 0.02 * nrm(ks[6], (DEPTH, 6, HEAD_DIM)),
        "sink": 0.5 * nrm(ks[7], (DEPTH, A_Q_HEADS)),
        "rpb": 0.1 * nrm(ks[8], (DEPTH, B_HEADS, 2 * NB_ROWS_MAX - 1, 2 * NB_COLS - 1)),
        "o_gain": 1.0 + 0.02 * nrm(ks[9], (DEPTH, MIX_W)),
        "w_out": nrm(ks[10], (DEPTH, MIX_W, D_MODEL)) * MIX_W ** -0.5,
        "g_mem": 1.0 + 0.02 * nrm(ks[11], (DEPTH, D_MODEL)),
        "w_mem_kv": nrm(ks[12], (DEPTH, D_MODEL, 2 * M_W)) * D_MODEL ** -0.5,
        "g_ff": 1.0 + 0.02 * nrm(ks[13], (DEPTH, D_MODEL)),
        "w_ff1": nrm(ks[14], (DEPTH, D_MODEL, D_FF)) * D_MODEL ** -0.5,
        "w_ff2": nrm(ks[15], (DEPTH, D_FF, D_MODEL)) * D_FF ** -0.5,
    }


def reference(x_prompt, x_sample, mem_prompt, mem_sample, g_mix, w_in, qk_gain, sink, rpb,
              o_gain, w_out, g_mem, w_mem_kv, g_ff, w_ff1, w_ff2):
    y_prompt = _trunk(x_prompt, mem_prompt, g_mix, w_in, qk_gain, sink, rpb, o_gain, w_out,
                      g_mem, w_mem_kv, g_ff, w_ff1, w_ff2)
    y_sample = _trunk(x_sample, mem_sample, g_mix, w_in, qk_gain, sink, rpb, o_gain, w_out,
                      g_mem, w_mem_kv, g_ff, w_ff1, w_ff2)
    return (y_prompt, y_sample)
```

```python
import functools

import numpy as np
import jax
import jax.numpy as jnp
from jax import lax
from jax.experimental import pallas as pl
from jax.experimental.pallas import tpu as pltpu

D_MODEL = 1024
DEPTH = 2
HEAD_DIM = 64
A_Q_HEADS = 8
A_KV_HEADS = 2
WINDOW = 128
A_BLOCK = 128
B_HEADS = 4
GRID_W = 64
NB_ROWS = 8
NB_COLS = 16
M_HEADS = 4
N_MEM = 256
D_FF = 4 * D_MODEL
EPS = 1e-6

A_Q = A_Q_HEADS * HEAD_DIM
A_KV = A_KV_HEADS * HEAD_DIM
B_W = B_HEADS * HEAD_DIM
M_W = M_HEADS * HEAD_DIM

Q_W = A_Q + B_W + M_W
NORMED_W = Q_W + 2 * 256
PROJ_W = NORMED_W + 2 * 256
LANE_TILE = 256
KV_W = 512

NAT_HALO = NB_ROWS * GRID_W
MASKED = 1e30

TOKENS_PER_TILE = 512
FF_CHUNK = 1024
VMEM_LIMIT_BYTES = 48 * 1024 * 1024

F32 = jnp.float32
BF16 = jnp.bfloat16


def _dot(a, b):
    return jnp.dot(a, b, preferred_element_type=F32)


def _dot_nt(a, b):
    return lax.dot_general(a, b, (((1,), (1,)), ((), ())), preferred_element_type=F32)


def _rms_scale(x):
    return x * lax.rsqrt(jnp.mean(x * x, axis=-1, keepdims=True) + EPS)


def _head_norm(p, ones, gain):
    ss = _dot((p * p).astype(BF16), ones)
    return p * lax.rsqrt(ss * (1.0 / HEAD_DIM) + EPS) * gain


def _lane_mask(width, lo, hi, dtype):
    lane = lax.broadcasted_iota(jnp.int32, (1, width), 1)
    return jnp.where((lane >= lo) & (lane < hi), 1.0, 0.0).astype(dtype)


def _mem_kv_kernel(mem_ref, g_ref, w_ref, gain_ref, ones_ref, out_ref):
    h = (_rms_scale(mem_ref[...]) * g_ref[...]).astype(BF16)
    mkv = _dot(h, w_ref[...])
    out_ref[:, 0:M_W] = _head_norm(mkv[:, 0:M_W], ones_ref[...], gain_ref[...]).astype(BF16)
    out_ref[:, M_W:2 * M_W] = mkv[:, M_W:2 * M_W].astype(BF16)


def _mem_kv(mem, g_mem, w_mem_kv, km_gain, ones):
    batch = mem.shape[0]
    return pl.pallas_call(
        _mem_kv_kernel,
        grid=(DEPTH, batch),
        in_specs=[
            pl.BlockSpec((None, N_MEM, D_MODEL), lambda l, b: (b, 0, 0)),
            pl.BlockSpec((None, 1, D_MODEL), lambda l, b: (l, 0, 0)),
            pl.BlockSpec((None, D_MODEL, 2 * M_W), lambda l, b: (l, 0, 0)),
            pl.BlockSpec((None, 1, M_W), lambda l, b: (l, 0, 0)),
            pl.BlockSpec((LANE_TILE, LANE_TILE), lambda l, b: (0, 0)),
        ],
        out_specs=pl.BlockSpec((None, None, N_MEM, 2 * M_W), lambda l, b: (l, b, 0, 0)),
        out_shape=jax.ShapeDtypeStruct((DEPTH, batch, N_MEM, 2 * M_W), BF16),
        compiler_params=pltpu.CompilerParams(dimension_semantics=("arbitrary", "arbitrary")),
        name="mem_kv",
    )(mem, g_mem, w_mem_kv, km_gain, ones)


def _in_proj_kernel(x_ref, g_ref, w_ref, gain_ref, ones_ref, q_ref, kva_ref, kvb_ref):
    h = (_rms_scale(x_ref[...]) * g_ref[...]).astype(BF16)
    proj = _dot(h, w_ref[...])
    ones = ones_ref[...]

    def normed(c):
        sl = slice(c * LANE_TILE, (c + 1) * LANE_TILE)
        return _head_norm(proj[:, sl], ones, gain_ref[:, sl]).astype(BF16)

    for c in range(Q_W // LANE_TILE):
        q_ref[:, c * LANE_TILE:(c + 1) * LANE_TILE] = normed(c)
    kva_ref[:, 0:256] = normed(4)
    kvb_ref[:, 0:256] = normed(5)
    kva_ref[:, 256:512] = proj[:, NORMED_W:NORMED_W + 256].astype(BF16)
    kvb_ref[:, 256:512] = proj[:, NORMED_W + 256:PROJ_W].astype(BF16)


def _in_proj(x2d, g_mix_l, w_in_l, gain_l, ones):
    tokens = x2d.shape[0]
    tm = TOKENS_PER_TILE
    const = lambda i: (0, 0)
    return pl.pallas_call(
        _in_proj_kernel,
        grid=(tokens // tm,),
        in_specs=[
            pl.BlockSpec((tm, D_MODEL), lambda i: (i, 0)),
            pl.BlockSpec((1, D_MODEL), const),
            pl.BlockSpec((D_MODEL, PROJ_W), const),
            pl.BlockSpec((1, NORMED_W), const),
            pl.BlockSpec((LANE_TILE, LANE_TILE), const),
        ],
        out_specs=[
            pl.BlockSpec((tm, Q_W), lambda i: (i, 0)),
            pl.BlockSpec((tm, KV_W), lambda i: (i, 0)),
            pl.BlockSpec((tm, KV_W), lambda i: (i, 0)),
        ],
        out_shape=[
            jax.ShapeDtypeStruct((tokens, Q_W), BF16),
            jax.ShapeDtypeStruct((tokens, KV_W), BF16),
            jax.ShapeDtypeStruct((tokens, KV_W), BF16),
        ],
        compiler_params=pltpu.CompilerParams(
            dimension_semantics=("arbitrary",), vmem_limit_bytes=VMEM_LIMIT_BYTES),
        name="in_proj",
    )(x2d, g_mix_l, w_in_l, gain_l, ones)


def _softmax_rows(s, extra_logit=None):
    m = jnp.max(s, axis=-1, keepdims=True)
    if extra_logit is not None:
        m = jnp.maximum(m, extra_logit)
    p = jnp.exp(s - m)
    l = jnp.sum(p, axis=-1, keepdims=True)
    if extra_logit is not None:
        l = l + jnp.exp(extra_logit - m)
    return p.astype(BF16), 1.0 / l


def _attn_kernel(seq, tq,
                 x_ref, q_ref, kap_ref, kac_ref, kan_ref, kbp_ref, kbc_ref, kbn_ref, kvm_ref,
                 dist_ref, nat_ref, sink_ref, og_ref, wout_ref, out_ref,
                 ka_buf, kb_buf, o_buf):
    i = pl.program_id(1)
    n_blocks = seq // A_BLOCK
    n_rows = seq // GRID_W
    blocks_per_tile = tq // A_BLOCK
    rows_per_tile = tq // GRID_W

    ka_buf[0:A_BLOCK] = kap_ref[...]
    ka_buf[A_BLOCK:A_BLOCK + tq] = kac_ref[...]
    ka_buf[A_BLOCK + tq:2 * A_BLOCK + tq] = kan_ref[...]
    kb_buf[0:NAT_HALO] = kbp_ref[...]
    kb_buf[NAT_HALO:NAT_HALO + tq] = kbc_ref[...]
    kb_buf[NAT_HALO + tq:2 * NAT_HALO + tq] = kbn_ref[...]

    lo_b = _lane_mask(128, 0, HEAD_DIM, BF16)
    hi_b = _lane_mask(128, HEAD_DIM, 128, BF16)

    def window_block(j, carry):
        blk = i * blocks_per_tile + j
        variant = jnp.where(blk == 0, 1, jnp.where(blk == n_blocks - 1, 2, 0))
        dist = dist_ref[variant]
        q0 = pl.multiple_of(j * A_BLOCK, A_BLOCK)
        for g in range(A_KV_HEADS):
            k2 = ka_buf[pl.ds(q0, 3 * A_BLOCK), 128 * g:128 * (g + 1)]
            v2 = ka_buf[pl.ds(q0, 3 * A_BLOCK), 256 + 128 * g:256 + 128 * (g + 1)]
            qg = q_ref[pl.ds(q0, A_BLOCK), 256 * g:256 * (g + 1)]
            lhs = jnp.concatenate([qg[:, 0:128], qg[:, 128:256]], axis=0)
            s_lo = _dot_nt(lhs, k2 * lo_b)
            s_hi = _dot_nt(lhs, k2 * hi_b)
            probs, recips = [], []
            for hh, s_all in ((0, s_lo), (1, s_hi), (2, s_lo), (3, s_hi)):
                head = 4 * g + hh
                s = s_all[0:A_BLOCK] if hh < 2 else s_all[A_BLOCK:2 * A_BLOCK]
                s = s - (2.0 ** -(head + 1)) * dist
                p, r = _softmax_rows(s, sink_ref[head])
                probs.append(p)
                recips.append(r)
            v_lo = v2 * lo_b
            v_hi = v2 * hi_b
            o_lo = _dot(jnp.concatenate([probs[0], probs[2]], axis=0), v_lo)
            o_hi = _dot(jnp.concatenate([probs[1], probs[3]], axis=0), v_hi)
            o_buf[pl.ds(q0, A_BLOCK), 256 * g:256 * g + 128] = (
                o_lo[0:A_BLOCK] * recips[0] + o_hi[0:A_BLOCK] * recips[1])
            o_buf[pl.ds(q0, A_BLOCK), 256 * g + 128:256 * (g + 1)] = (
                o_lo[A_BLOCK:] * recips[2] + o_hi[A_BLOCK:] * recips[3])
        return carry

    lax.fori_loop(0, blocks_per_tile, window_block, 0)

    head_b = [_lane_mask(B_W, HEAD_DIM * h, HEAD_DIM * (h + 1), BF16) for h in range(B_HEADS)]
    head_f = [_lane_mask(B_W, HEAD_DIM * h, HEAD_DIM * (h + 1), F32) for h in range(B_HEADS)]

    def nat_row(rr, carry):
        r = i * rows_per_tile + rr
        start = jnp.clip(r - NB_ROWS // 2, 0, n_rows - NB_ROWS)
        variant = r - start
        k0 = pl.multiple_of((start - i * rows_per_tile + NB_ROWS) * GRID_W, GRID_W)
        q0 = pl.multiple_of(rr * GRID_W, GRID_W)
        kwin = kb_buf[pl.ds(k0, NAT_HALO), 0:B_W]
        vwin = kb_buf[pl.ds(k0, NAT_HALO), B_W:2 * B_W]
        qrow = q_ref[pl.ds(q0, GRID_W), A_Q:A_Q + B_W]
        lhs = jnp.concatenate([qrow * head_b[h] for h in range(B_HEADS)], axis=0)
        s = _dot_nt(lhs, kwin) + nat_ref[variant]
        p, rl = _softmax_rows(s)
        pv = _dot(p, vwin) * rl
        o = pv[0:GRID_W] * head_f[0]
        for h in range(1, B_HEADS):
            o = o + pv[GRID_W * h:GRID_W * (h + 1)] * head_f[h]
        o_buf[pl.ds(q0, GRID_W), A_Q:A_Q + B_W] = o
        return carry

    lax.fori_loop(0, rows_per_tile, nat_row, 0)

    km = kvm_ref[:, 0:M_W]
    vm = kvm_ref[:, M_W:2 * M_W]
    qm = q_ref[:, A_Q + B_W:Q_W]
    om = jnp.zeros((tq, M_W), F32)
    for h in range(M_HEADS):
        p, rl = _softmax_rows(_dot_nt(qm, km * head_b[h]))
        om = om + _dot(p, vm * head_b[h]) * rl
    o_buf[:, A_Q + B_W:Q_W] = om

    o = jnp.concatenate(
        [_rms_scale(o_buf[:, 0:A_Q]), _rms_scale(o_buf[:, A_Q:A_Q + B_W]), _rms_scale(o_buf[:, A_Q + B_W:Q_W])],
        axis=-1) * og_ref[...]
    out_ref[...] = x_ref[...] + _dot(o.astype(BF16), wout_ref[...])


def _attention(x, q, kva, kvb, kvm_l, dist_tab, nat_tab, sink_l, og_l, wout_l):
    batch, seq, _ = x.shape
    tq = TOKENS_PER_TILE
    assert seq % tq == 0 and tq % NAT_HALO == 0 and seq >= 2 * A_BLOCK and seq >= NB_ROWS * GRID_W
    nt = seq // tq
    a_per_tile = tq // A_BLOCK
    n_ablocks = seq // A_BLOCK
    b_per_tile = tq // NAT_HALO
    n_bblocks = seq // NAT_HALO
    const2 = lambda b, i: (0, 0)
    const3 = lambda b, i: (0, 0, 0)
    return pl.pallas_call(
        functools.partial(_attn_kernel, seq, tq),
        grid=(batch, nt),
        in_specs=[
            pl.BlockSpec((None, tq, D_MODEL), lambda b, i: (b, i, 0)),
            pl.BlockSpec((None, tq, Q_W), lambda b, i: (b, i, 0)),
            pl.BlockSpec((None, A_BLOCK, KV_W), lambda b, i: (b, jnp.maximum(i * a_per_tile - 1, 0), 0)),
            pl.BlockSpec((None, tq, KV_W), lambda b, i: (b, i, 0)),
            pl.BlockSpec((None, A_BLOCK, KV_W), lambda b, i: (b, jnp.minimum((i + 1) * a_per_tile, n_ablocks - 1), 0)),
            pl.BlockSpec((None, NAT_HALO, KV_W), lambda b, i: (b, jnp.maximum(i * b_per_tile - 1, 0), 0)),
            pl.BlockSpec((None, tq, KV_W), lambda b, i: (b, i, 0)),
            pl.BlockSpec((None, NAT_HALO, KV_W), lambda b, i: (b, jnp.minimum((i + 1) * b_per_tile, n_bblocks - 1), 0)),
            pl.BlockSpec((None, N_MEM, 2 * M_W), lambda b, i: (b, 0, 0)),
            pl.BlockSpec((3, A_BLOCK, 3 * A_BLOCK), const3),
            pl.BlockSpec((NB_ROWS, B_HEADS * GRID_W, NB_ROWS * GRID_W), const3),
            pl.BlockSpec(memory_space=pltpu.SMEM),
            pl.BlockSpec((1, Q_W), const2),
            pl.BlockSpec((Q_W, D_MODEL), const2),
        ],
        out_specs=pl.BlockSpec((None, tq, D_MODEL), lambda b, i: (b, i, 0)),
        out_shape=jax.ShapeDtypeStruct((batch, seq, D_MODEL), F32),
        scratch_shapes=[
            pltpu.VMEM((tq + 2 * A_BLOCK, KV_W), BF16),
            pltpu.VMEM((tq + 2 * NAT_HALO, KV_W), BF16),
            pltpu.VMEM((tq, Q_W), F32),
        ],
        compiler_params=pltpu.CompilerParams(
            dimension_semantics=("arbitrary", "arbitrary"), vmem_limit_bytes=VMEM_LIMIT_BYTES),
        name="attention",
    )(x, q, kva, kva, kva, kvb, kvb, kvb, kvm_l, dist_tab, nat_tab, sink_l, og_l, wout_l)


def _ffn_kernel(x_ref, g_ref, w1_ref, w2_ref, out_ref):
    x = x_ref[...]
    h = (_rms_scale(x) * g_ref[...]).astype(BF16)
    acc = x
    for c in range(D_FF // FF_CHUNK):
        f = _dot(h, w1_ref[:, c * FF_CHUNK:(c + 1) * FF_CHUNK])
        a = jnp.square(jnp.maximum(f, 0.0)).astype(BF16)
        acc = acc + _dot(a, w2_ref[c * FF_CHUNK:(c + 1) * FF_CHUNK, :])
    out_ref[...] = acc


def _ffn(x2d, g_ff_l, w1_l, w2_l):
    tokens = x2d.shape[0]
    tm = TOKENS_PER_TILE
    const = lambda i: (0, 0)
    return pl.pallas_call(
        _ffn_kernel,
        grid=(tokens // tm,),
        in_specs=[
            pl.BlockSpec((tm, D_MODEL), lambda i: (i, 0)),
            pl.BlockSpec((1, D_MODEL), const),
            pl.BlockSpec((D_MODEL, D_FF), const, pipeline_mode=pl.Buffered(1)),
            pl.BlockSpec((D_FF, D_MODEL), const, pipeline_mode=pl.Buffered(1)),
        ],
        out_specs=pl.BlockSpec((tm, D_MODEL), lambda i: (i, 0)),
        out_shape=jax.ShapeDtypeStruct((tokens, D_MODEL), F32),
        compiler_params=pltpu.CompilerParams(
            dimension_semantics=("arbitrary",), vmem_limit_bytes=VMEM_LIMIT_BYTES),
        name="ffn",
    )(x2d, g_ff_l, w1_l, w2_l)


def _window_distance_table():
    qi = np.arange(A_BLOCK)[:, None]
    si = np.arange(3 * A_BLOCK)[None, :]
    dist = np.abs(si - A_BLOCK - qi).astype(np.float32)
    band = dist <= WINDOW
    keep = [band, band & (si >= A_BLOCK), band & (si < 2 * A_BLOCK)]
    return jnp.asarray(np.stack([np.where(k, dist, MASKED) for k in keep]).astype(np.float32))


def _neighbourhood_bias_table(rpb_l):
    variant = np.arange(NB_ROWS)[:, None]
    krow = np.arange(NB_ROWS)[None, :]
    dr = krow - variant + (NB_ROWS - 1)
    c = np.arange(GRID_W)
    dc = np.clip(c[None, :] - c[:, None] + (NB_COLS - 1), 0, 2 * NB_COLS - 2)
    cstart = np.clip(c - NB_COLS // 2, 0, GRID_W - NB_COLS)
    cvalid = (c[None, :] >= cstart[:, None]) & (c[None, :] < cstart[:, None] + NB_COLS)
    t = rpb_l.astype(F32)[:, dr[:, :, None, None], dc[None, None, :, :]]
    t = jnp.where(cvalid[None, None, None], t, -MASKED)
    t = jnp.transpose(t, (1, 0, 3, 2, 4))
    return t.reshape(NB_ROWS, B_HEADS * GRID_W, NB_ROWS * GRID_W)


def _pack_in_proj(w_in_l):
    qa, ka, va, qb, kb, vb, qm = jnp.split(
        w_in_l, [A_Q, A_Q + A_KV, A_Q + 2 * A_KV, A_Q + 2 * A_KV + B_W,
                 A_Q + 2 * A_KV + 2 * B_W, A_Q + 2 * A_KV + 3 * B_W], axis=1)
    twice = lambda t: jnp.concatenate(
        [t[:, :HEAD_DIM], t[:, :HEAD_DIM], t[:, HEAD_DIM:], t[:, HEAD_DIM:]], axis=1)
    return jnp.concatenate([qa, qb, qm, twice(ka), kb, twice(va), vb], axis=1).astype(BF16)


def _pack_qk_gain(qk_gain_l):
    scale = HEAD_DIM ** -0.5
    g = qk_gain_l.astype(F32)
    return jnp.concatenate([
        jnp.tile(g[0], A_Q_HEADS) * scale, jnp.tile(g[2], B_HEADS) * scale, jnp.tile(g[4], M_HEADS) * scale,
        jnp.tile(g[1], 2 * A_KV_HEADS), jnp.tile(g[3], B_HEADS)])[None, :]


def _trunk(x, kvm, params):
    batch, seq, _ = x.shape
    for l in range(DEPTH):
        p = params[l]
        q, kva, kvb = _in_proj(x.reshape(batch * seq, D_MODEL), p["g_mix"], p["w_in"], p["qk_gain"], params["ones"])
        x = _attention(x, q.reshape(batch, seq, Q_W), kva.reshape(batch, seq, KV_W), kvb.reshape(batch, seq, KV_W),
                       kvm[l], params["dist"], p["nat"], p["sink"], p["o_gain"], p["w_out"])
        x = _ffn(x.reshape(batch * seq, D_MODEL), p["g_ff"], p["w_ff1"], p["w_ff2"]).reshape(batch, seq, D_MODEL)
    return x


def kernel(x_prompt, x_sample, mem_prompt, mem_sample, g_mix, w_in, qk_gain, sink, rpb,
           o_gain, w_out, g_mem, w_mem_kv, g_ff, w_ff1, w_ff2):
    ones = jnp.asarray(np.kron(np.eye(LANE_TILE // HEAD_DIM), np.ones((HEAD_DIM, HEAD_DIM))), BF16)
    params = {"ones": ones, "dist": _window_distance_table()}
    for l in range(DEPTH):
        params[l] = {
            "g_mix": g_mix[l].astype(F32)[None, :],
            "w_in": _pack_in_proj(w_in[l]),
            "qk_gain": _pack_qk_gain(qk_gain[l]),
            "nat": _neighbourhood_bias_table(rpb[l]),
            "sink": sink[l].astype(F32),
            "o_gain": o_gain[l].astype(F32)[None, :],
            "w_out": w_out[l].astype(BF16),
            "g_ff": g_ff[l].astype(F32)[None, :],
            "w_ff1": w_ff1[l].astype(BF16),
            "w_ff2": w_ff2[l].astype(BF16),
        }
    km_gain = jnp.tile(qk_gain[:, 5].astype(F32), (1, M_HEADS))[:, None, :]
    g_mem3 = g_mem.astype(F32)[:, None, :]
    w_mem = w_mem_kv.astype(BF16)
    kvm_prompt = _mem_kv(mem_prompt, g_mem3, w_mem, km_gain, ones)
    kvm_sample = _mem_kv(mem_sample, g_mem3, w_mem, km_gain, ones)
    return (_trunk(x_prompt, kvm_prompt, params), _trunk(x_sample, kvm_sample, params))
```

```python
import functools

import numpy as np
import jax
import jax.numpy as jnp
from jax import lax
from jax.experimental import pallas as pl
from jax.experimental.pallas import tpu as pltpu

D_MODEL = 1024
DEPTH = 2
HEAD_DIM = 64
A_Q_HEADS = 8
A_KV_HEADS = 2
WINDOW = 128
A_BLOCK = 128
B_HEADS = 4
GRID_W = 64
NB_ROWS = 8
NB_COLS = 16
M_HEADS = 4
N_MEM = 256
D_FF = 4 * D_MODEL
EPS = 1e-6

A_Q = A_Q_HEADS * HEAD_DIM
A_KV = A_KV_HEADS * HEAD_DIM
B_W = B_HEADS * HEAD_DIM
M_W = M_HEADS * HEAD_DIM

Q_W = A_Q + B_W + M_W
NORMED_W = Q_W + 2 * 256
PROJ_W = NORMED_W + 2 * 256
LANE_TILE = 256
KV_W = 512

NAT_HALO = NB_ROWS * GRID_W
MASKED = 1e30
LOG2E = float(np.log2(np.e))

TOKENS_PER_TILE = 512
FF_CHUNK = 1024
VMEM_LIMIT_BYTES = 48 * 1024 * 1024

F32 = jnp.float32
BF16 = jnp.bfloat16


def _dot(a, b):
    return jnp.dot(a, b, preferred_element_type=F32)


def _dot_nt(a, b):
    return lax.dot_general(a, b, (((1,), (1,)), ((), ())), preferred_element_type=F32)


def _rms_scale(x):
    return x * lax.rsqrt(jnp.mean(x * x, axis=-1, keepdims=True) + EPS)


def _head_norm(p, ones, gain):
    ss = _dot((p * p).astype(BF16), ones)
    return p * lax.rsqrt(ss * (1.0 / HEAD_DIM) + EPS) * gain


def _lane_mask(width, lo, hi, dtype):
    lane = lax.broadcasted_iota(jnp.int32, (1, width), 1)
    return jnp.where((lane >= lo) & (lane < hi), 1.0, 0.0).astype(dtype)


def _mem_kv_kernel(mem_ref, g_ref, w_ref, gain_ref, ones_ref, out_ref):
    h = (_rms_scale(mem_ref[...]) * g_ref[...]).astype(BF16)
    mkv = _dot(h, w_ref[...])
    out_ref[:, 0:M_W] = _head_norm(mkv[:, 0:M_W], ones_ref[...], gain_ref[...]).astype(BF16)
    out_ref[:, M_W:2 * M_W] = mkv[:, M_W:2 * M_W].astype(BF16)


def _mem_kv(mem, g_mem, w_mem_kv, km_gain, ones):
    batch = mem.shape[0]
    return pl.pallas_call(
        _mem_kv_kernel,
        grid=(DEPTH, batch),
        in_specs=[
            pl.BlockSpec((None, N_MEM, D_MODEL), lambda l, b: (b, 0, 0)),
            pl.BlockSpec((None, 1, D_MODEL), lambda l, b: (l, 0, 0)),
            pl.BlockSpec((None, D_MODEL, 2 * M_W), lambda l, b: (l, 0, 0)),
            pl.BlockSpec((None, 1, M_W), lambda l, b: (l, 0, 0)),
            pl.BlockSpec((LANE_TILE, LANE_TILE), lambda l, b: (0, 0)),
        ],
        out_specs=pl.BlockSpec((None, None, N_MEM, 2 * M_W), lambda l, b: (l, b, 0, 0)),
        out_shape=jax.ShapeDtypeStruct((DEPTH, batch, N_MEM, 2 * M_W), BF16),
        compiler_params=pltpu.CompilerParams(dimension_semantics=("arbitrary", "arbitrary")),
        name="mem_kv",
    )(mem, g_mem, w_mem_kv, km_gain, ones)


def _in_proj_kernel(x_ref, g_ref, w_ref, gain_ref, ones_ref, q_ref, kva_ref, kvb_ref):
    h = (_rms_scale(x_ref[...]) * g_ref[...]).astype(BF16)
    proj = _dot(h, w_ref[...])
    ones = ones_ref[...]

    def normed(c):
        sl = slice(c * LANE_TILE, (c + 1) * LANE_TILE)
        return _head_norm(proj[:, sl], ones, gain_ref[:, sl]).astype(BF16)

    for c in range(Q_W // LANE_TILE):
        q_ref[:, c * LANE_TILE:(c + 1) * LANE_TILE] = normed(c)
    kva_ref[:, 0:256] = normed(4)
    kvb_ref[:, 0:256] = normed(5)
    kva_ref[:, 256:512] = proj[:, NORMED_W:NORMED_W + 256].astype(BF16)
    kvb_ref[:, 256:512] = proj[:, NORMED_W + 256:PROJ_W].astype(BF16)


def _in_proj(x2d, g_mix_l, w_in_l, gain_l, ones):
    tokens = x2d.shape[0]
    tm = TOKENS_PER_TILE
    const = lambda i: (0, 0)
    return pl.pallas_call(
        _in_proj_kernel,
        grid=(tokens // tm,),
        in_specs=[
            pl.BlockSpec((tm, D_MODEL), lambda i: (i, 0)),
            pl.BlockSpec((1, D_MODEL), const),
            pl.BlockSpec((D_MODEL, PROJ_W), const),
            pl.BlockSpec((1, NORMED_W), const),
            pl.BlockSpec((LANE_TILE, LANE_TILE), const),
        ],
        out_specs=[
            pl.BlockSpec((tm, Q_W), lambda i: (i, 0)),
            pl.BlockSpec((tm, KV_W), lambda i: (i, 0)),
            pl.BlockSpec((tm, KV_W), lambda i: (i, 0)),
        ],
        out_shape=[
            jax.ShapeDtypeStruct((tokens, Q_W), BF16),
            jax.ShapeDtypeStruct((tokens, KV_W), BF16),
            jax.ShapeDtypeStruct((tokens, KV_W), BF16),
        ],
        compiler_params=pltpu.CompilerParams(
            dimension_semantics=("arbitrary",), vmem_limit_bytes=VMEM_LIMIT_BYTES),
        name="in_proj",
    )(x2d, g_mix_l, w_in_l, gain_l, ones)


def _attn_kernel(seq, tq,
                 x_ref, q_ref, kap_ref, kac_ref, kan_ref, kbp_ref, kbc_ref, kbn_ref, kvm_ref,
                 wbias_ref, nat_ref, sink_ref, og_ref, wout_ref, out_ref,
                 kw_buf, vw_buf, kb_buf, o_buf):
    i = pl.program_id(1)
    n_blocks = seq // A_BLOCK
    n_rows = seq // GRID_W
    blocks_per_tile = tq // A_BLOCK
    rows_per_tile = tq // GRID_W

    lo_b = _lane_mask(128, 0, HEAD_DIM, BF16)
    hi_b = _lane_mask(128, HEAD_DIM, 128, BF16)
    for src, r0, n in ((kap_ref, 0, A_BLOCK), (kac_ref, A_BLOCK, tq), (kan_ref, A_BLOCK + tq, A_BLOCK)):
        rows = slice(r0, r0 + n)
        for g in range(A_KV_HEADS):
            k2 = src[:, 128 * g:128 * (g + 1)]
            v2 = src[:, 256 + 128 * g:256 + 128 * (g + 1)]
            kw_buf[rows, 256 * g:256 * g + 128] = k2 * lo_b
            kw_buf[rows, 256 * g + 128:256 * (g + 1)] = k2 * hi_b
            vw_buf[rows, 512 * g:512 * g + 128] = v2 * lo_b
            vw_buf[rows, 512 * g + 128:512 * g + 256] = jnp.broadcast_to(_lane_mask(128, 0, 1, BF16), (n, 128))
            vw_buf[rows, 512 * g + 256:512 * g + 384] = v2 * hi_b
            vw_buf[rows, 512 * g + 384:512 * (g + 1)] = jnp.broadcast_to(_lane_mask(128, 1, 2, BF16), (n, 128))
    kb_buf[0:NAT_HALO] = kbp_ref[...]
    kb_buf[NAT_HALO:NAT_HALO + tq] = kbc_ref[...]
    kb_buf[NAT_HALO + tq:2 * NAT_HALO + tq] = kbn_ref[...]

    top_rows = lax.broadcasted_iota(jnp.int32, (2 * A_BLOCK, 1), 0) < A_BLOCK
    low_lanes = lax.broadcasted_iota(jnp.int32, (1, 128), 1) < HEAD_DIM

    def window_block(j, carry):
        blk = i * blocks_per_tile + j
        variant = jnp.where(blk == 0, 1, jnp.where(blk == n_blocks - 1, 2, 0))
        q0 = pl.multiple_of(j * A_BLOCK, A_BLOCK)
        band = pl.ds(q0, 3 * A_BLOCK)
        for g in range(A_KV_HEADS):
            qg = q_ref[pl.ds(q0, A_BLOCK), 256 * g:256 * (g + 1)]
            lhs = jnp.concatenate([qg[:, 0:128], qg[:, 128:256]], axis=0)
            keys = jnp.concatenate(
                [kw_buf[band, 256 * g:256 * g + 128], kw_buf[band, 256 * g + 128:256 * (g + 1)]], axis=0)
            s = _dot_nt(lhs, keys) + wbias_ref[variant, g]
            sink_lo = jnp.where(top_rows, sink_ref[4 * g], sink_ref[4 * g + 2])
            sink_hi = jnp.where(top_rows, sink_ref[4 * g + 1], sink_ref[4 * g + 3])
            s_lo = s[:, 0:3 * A_BLOCK]
            s_hi = s[:, 3 * A_BLOCK:6 * A_BLOCK]
            m_lo = jnp.maximum(jnp.max(s_lo, axis=-1, keepdims=True), sink_lo)
            m_hi = jnp.maximum(jnp.max(s_hi, axis=-1, keepdims=True), sink_hi)
            p = jnp.concatenate([jnp.exp2(s_lo - m_lo), jnp.exp2(s_hi - m_hi)], axis=1).astype(BF16)
            vals = jnp.concatenate(
                [vw_buf[band, 512 * g:512 * g + 256], vw_buf[band, 512 * g + 256:512 * (g + 1)]], axis=0)
            o = _dot(p, vals)
            l_lo = o[:, 128:129] + jnp.exp2(sink_lo - m_lo)
            l_hi = o[:, 129:130] + jnp.exp2(sink_hi - m_hi)
            res = o[:, 0:128] * jnp.where(low_lanes, 1.0 / l_lo, 1.0 / l_hi)
            o_buf[pl.ds(q0, A_BLOCK), 256 * g:256 * g + 128] = res[0:A_BLOCK]
            o_buf[pl.ds(q0, A_BLOCK), 256 * g + 128:256 * (g + 1)] = res[A_BLOCK:2 * A_BLOCK]
        return carry

    lax.fori_loop(0, blocks_per_tile, window_block, 0, unroll=2)

    head_b = [_lane_mask(B_W, HEAD_DIM * h, HEAD_DIM * (h + 1), BF16) for h in range(B_HEADS)]
    head_f = [_lane_mask(B_W, HEAD_DIM * h, HEAD_DIM * (h + 1), F32) for h in range(B_HEADS)]

    def nat_row(rr, carry):
        r = i * rows_per_tile + rr
        start = jnp.clip(r - NB_ROWS // 2, 0, n_rows - NB_ROWS)
        d0 = (NB_ROWS - 1) - (r - start)
        k0 = pl.multiple_of((start - i * rows_per_tile + NB_ROWS) * GRID_W, GRID_W)
        q0 = pl.multiple_of(rr * GRID_W, GRID_W)
        kwin = kb_buf[pl.ds(k0, NAT_HALO), 0:B_W]
        vwin = kb_buf[pl.ds(k0, NAT_HALO), B_W:2 * B_W]
        qrow = q_ref[pl.ds(q0, GRID_W), A_Q:A_Q + B_W]
        lhs = jnp.concatenate([qrow * head_b[h] for h in range(B_HEADS)], axis=0)
        bias = jnp.concatenate(
            [jnp.concatenate([nat_ref[h, d0 + 2 * m] for m in range(NB_ROWS // 2)], axis=1)
             for h in range(B_HEADS)], axis=0)
        s = _dot_nt(lhs, kwin) + bias
        p = jnp.exp2(s - jnp.max(s, axis=-1, keepdims=True))
        rl = 1.0 / jnp.sum(p, axis=-1, keepdims=True)
        pv = _dot(p.astype(BF16), vwin) * rl
        o = pv[0:GRID_W] * head_f[0]
        for h in range(1, B_HEADS):
            o = o + pv[GRID_W * h:GRID_W * (h + 1)] * head_f[h]
        o_buf[pl.ds(q0, GRID_W), A_Q:A_Q + B_W] = o
        return carry

    lax.fori_loop(0, rows_per_tile, nat_row, 0, unroll=2)

    km = kvm_ref[:, 0:M_W]
    vm = kvm_ref[:, M_W:2 * M_W]
    qm = q_ref[:, A_Q + B_W:Q_W]
    s = _dot_nt(qm, jnp.concatenate([km * head_b[h] for h in range(M_HEADS)], axis=0))
    probs = []
    scale = jnp.zeros((tq, M_W), F32)
    for h in range(M_HEADS):
        s_h = s[:, N_MEM * h:N_MEM * (h + 1)]
        p = jnp.exp2(s_h - jnp.max(s_h, axis=-1, keepdims=True))
        scale = scale + head_f[h] * (1.0 / jnp.sum(p, axis=-1, keepdims=True))
        probs.append(p.astype(BF16))
    om = _dot(jnp.concatenate(probs, axis=1),
              jnp.concatenate([vm * head_b[h] for h in range(M_HEADS)], axis=0))
    o_buf[:, A_Q + B_W:Q_W] = om * scale

    o = jnp.concatenate(
        [_rms_scale(o_buf[:, 0:A_Q]), _rms_scale(o_buf[:, A_Q:A_Q + B_W]), _rms_scale(o_buf[:, A_Q + B_W:Q_W])],
        axis=-1) * og_ref[...]
    out_ref[...] = x_ref[...] + _dot(o.astype(BF16), wout_ref[...])


def _attention(x, q, kva, kvb, kvm_l, wbias_tab, nat_tab, sink_l, og_l, wout_l):
    batch, seq, _ = x.shape
    tq = TOKENS_PER_TILE
    assert seq % tq == 0 and tq % NAT_HALO == 0 and seq >= 2 * A_BLOCK and seq >= NB_ROWS * GRID_W
    nt = seq // tq
    a_per_tile = tq // A_BLOCK
    n_ablocks = seq // A_BLOCK
    b_per_tile = tq // NAT_HALO
    n_bblocks = seq // NAT_HALO
    const2 = lambda b, i: (0, 0)
    const4 = lambda b, i: (0, 0, 0, 0)
    once = pl.Buffered(1)
    return pl.pallas_call(
        functools.partial(_attn_kernel, seq, tq),
        grid=(batch, nt),
        in_specs=[
            pl.BlockSpec((None, tq, D_MODEL), lambda b, i: (b, i, 0)),
            pl.BlockSpec((None, tq, Q_W), lambda b, i: (b, i, 0)),
            pl.BlockSpec((None, A_BLOCK, KV_W), lambda b, i: (b, jnp.maximum(i * a_per_tile - 1, 0), 0)),
            pl.BlockSpec((None, tq, KV_W), lambda b, i: (b, i, 0)),
            pl.BlockSpec((None, A_BLOCK, KV_W), lambda b, i: (b, jnp.minimum((i + 1) * a_per_tile, n_ablocks - 1), 0)),
            pl.BlockSpec((None, NAT_HALO, KV_W), lambda b, i: (b, jnp.maximum(i * b_per_tile - 1, 0), 0)),
            pl.BlockSpec((None, tq, KV_W), lambda b, i: (b, i, 0)),
            pl.BlockSpec((None, NAT_HALO, KV_W), lambda b, i: (b, jnp.minimum((i + 1) * b_per_tile, n_bblocks - 1), 0)),
            pl.BlockSpec((None, N_MEM, 2 * M_W), lambda b, i: (b, 0, 0)),
            pl.BlockSpec((3, A_KV_HEADS, 2 * A_BLOCK, 6 * A_BLOCK), const4, pipeline_mode=once),
            pl.BlockSpec((B_HEADS, 2 * NB_ROWS - 2, GRID_W, 2 * GRID_W), const4, pipeline_mode=once),
            pl.BlockSpec(memory_space=pltpu.SMEM),
            pl.BlockSpec((1, Q_W), const2),
            pl.BlockSpec((Q_W, D_MODEL), const2, pipeline_mode=once),
        ],
        out_specs=pl.BlockSpec((None, tq, D_MODEL), lambda b, i: (b, i, 0)),
        out_shape=jax.ShapeDtypeStruct((batch, seq, D_MODEL), F32),
        scratch_shapes=[
            pltpu.VMEM((tq + 2 * A_BLOCK, 2 * 256), BF16),
            pltpu.VMEM((tq + 2 * A_BLOCK, 2 * 512), BF16),
            pltpu.VMEM((tq + 2 * NAT_HALO, KV_W), BF16),
            pltpu.VMEM((tq, Q_W), F32),
        ],
        compiler_params=pltpu.CompilerParams(
            dimension_semantics=("arbitrary", "arbitrary"), vmem_limit_bytes=VMEM_LIMIT_BYTES),
        name="attention",
    )(x, q, kva, kva, kva, kvb, kvb, kvb, kvm_l, wbias_tab, nat_tab, sink_l, og_l, wout_l)


def _ffn_kernel(x_ref, g_ref, w1_ref, w2_ref, out_ref):
    x = x_ref[...]
    h = (_rms_scale(x) * g_ref[...]).astype(BF16)
    acc = x
    for c in range(D_FF // FF_CHUNK):
        f = _dot(h, w1_ref[:, c * FF_CHUNK:(c + 1) * FF_CHUNK])
        a = jnp.square(jnp.maximum(f, 0.0)).astype(BF16)
        acc = acc + _dot(a, w2_ref[c * FF_CHUNK:(c + 1) * FF_CHUNK, :])
    out_ref[...] = acc


def _ffn(x2d, g_ff_l, w1_l, w2_l):
    tokens = x2d.shape[0]
    tm = TOKENS_PER_TILE
    const = lambda i: (0, 0)
    return pl.pallas_call(
        _ffn_kernel,
        grid=(tokens // tm,),
        in_specs=[
            pl.BlockSpec((tm, D_MODEL), lambda i: (i, 0)),
            pl.BlockSpec((1, D_MODEL), const),
            pl.BlockSpec((D_MODEL, D_FF), const, pipeline_mode=pl.Buffered(1)),
            pl.BlockSpec((D_FF, D_MODEL), const, pipeline_mode=pl.Buffered(1)),
        ],
        out_specs=pl.BlockSpec((tm, D_MODEL), lambda i: (i, 0)),
        out_shape=jax.ShapeDtypeStruct((tokens, D_MODEL), F32),
        compiler_params=pltpu.CompilerParams(
            dimension_semantics=("arbitrary",), vmem_limit_bytes=VMEM_LIMIT_BYTES),
        name="ffn",
    )(x2d, g_ff_l, w1_l, w2_l)


def _window_bias_table():
    qi = np.arange(A_BLOCK)[:, None]
    si = np.arange(3 * A_BLOCK)[None, :]
    dist = np.abs(si - A_BLOCK - qi).astype(np.float32)
    band = dist <= WINDOW
    keep = [band, band & (si >= A_BLOCK), band & (si < 2 * A_BLOCK)]
    slopes = (2.0 ** (-8.0 * np.arange(1, A_Q_HEADS + 1) / A_Q_HEADS)).astype(np.float32)
    tab = np.empty((3, A_KV_HEADS, 2 * A_BLOCK, 6 * A_BLOCK), np.float32)
    for v in range(3):
        for g in range(A_KV_HEADS):
            for hh in range(4):
                rows = slice(A_BLOCK * (hh // 2), A_BLOCK * (hh // 2 + 1))
                cols = slice(3 * A_BLOCK * (hh % 2), 3 * A_BLOCK * (hh % 2 + 1))
                tab[v, g, rows, cols] = np.where(keep[v], -(slopes[4 * g + hh] * dist) * LOG2E, -MASKED)
    return jnp.asarray(tab)


def _neighbourhood_bias_table(rpb_l):
    c = np.arange(GRID_W)
    dc = np.clip(c[None, :] - c[:, None] + (NB_COLS - 1), 0, 2 * NB_COLS - 2)
    onehot = jnp.asarray((dc[None] == np.arange(2 * NB_COLS - 1)[:, None, None]).astype(np.float32))
    t = jnp.einsum("hdc,cqk->hdqk", rpb_l.astype(F32), onehot, precision=lax.Precision.HIGHEST) * LOG2E
    cstart = np.clip(c - NB_COLS // 2, 0, GRID_W - NB_COLS)
    cvalid = (c[None, :] >= cstart[:, None]) & (c[None, :] < cstart[:, None] + NB_COLS)
    t = jnp.where(cvalid[None, None], t, -MASKED)
    return jnp.concatenate([t[:, :-1], t[:, 1:]], axis=-1)


def _pack_in_proj(w_in_l):
    qa, ka, va, qb, kb, vb, qm = jnp.split(
        w_in_l, [A_Q, A_Q + A_KV, A_Q + 2 * A_KV, A_Q + 2 * A_KV + B_W,
                 A_Q + 2 * A_KV + 2 * B_W, A_Q + 2 * A_KV + 3 * B_W], axis=1)
    twice = lambda t: jnp.concatenate(
        [t[:, :HEAD_DIM], t[:, :HEAD_DIM], t[:, HEAD_DIM:], t[:, HEAD_DIM:]], axis=1)
    return jnp.concatenate([qa, qb, qm, twice(ka), kb, twice(va), vb], axis=1).astype(BF16)


def _pack_qk_gain(qk_gain_l):
    scale = HEAD_DIM ** -0.5 * LOG2E
    g = qk_gain_l.astype(F32)
    return jnp.concatenate([
        jnp.tile(g[0], A_Q_HEADS) * scale, jnp.tile(g[2], B_HEADS) * scale, jnp.tile(g[4], M_HEADS) * scale,
        jnp.tile(g[1], 2 * A_KV_HEADS), jnp.tile(g[3], B_HEADS)])[None, :]


def _trunk(x, kvm, params):
    batch, seq, _ = x.shape
    for l in range(DEPTH):
        p = params[l]
        q, kva, kvb = _in_proj(x.reshape(batch * seq, D_MODEL), p["g_mix"], p["w_in"], p["qk_gain"], params["ones"])
        x = _attention(x, q.reshape(batch, seq, Q_W), kva.reshape(batch, seq, KV_W), kvb.reshape(batch, seq, KV_W),
                       kvm[l], params["wbias"], p["nat"], p["sink"], p["o_gain"], p["w_out"])
        x = _ffn(x.reshape(batch * seq, D_MODEL), p["g_ff"], p["w_ff1"], p["w_ff2"]).reshape(batch, seq, D_MODEL)
    return x


def kernel(x_prompt, x_sample, mem_prompt, mem_sample, g_mix, w_in, qk_gain, sink, rpb,
           o_gain, w_out, g_mem, w_mem_kv, g_ff, w_ff1, w_ff2):
    ones = jnp.asarray(np.kron(np.eye(LANE_TILE // HEAD_DIM), np.ones((HEAD_DIM, HEAD_DIM))), BF16)
    params = {"ones": ones, "wbias": _window_bias_table()}
    for l in range(DEPTH):
        params[l] = {
            "g_mix": g_mix[l].astype(F32)[None, :],
            "w_in": _pack_in_proj(w_in[l]),
            "qk_gain": _pack_qk_gain(qk_gain[l]),
            "nat": _neighbourhood_bias_table(rpb[l]),
            "sink": sink[l].astype(F32) * LOG2E,
            "o_gain": o_gain[l].astype(F32)[None, :],
            "w_out": w_out[l].astype(BF16),
            "g_ff": g_ff[l].astype(F32)[None, :],
            "w_ff1": w_ff1[l].astype(BF16),
            "w_ff2": w_ff2[l].astype(BF16),
        }
    km_gain = jnp.tile(qk_gain[:, 5].astype(F32), (1, M_HEADS))[:, None, :]
    g_mem3 = g_mem.astype(F32)[:, None, :]
    w_mem = w_mem_kv.astype(BF16)
    kvm_prompt = _mem_kv(mem_prompt, g_mem3, w_mem, km_gain, ones)
    kvm_sample = _mem_kv(mem_sample, g_mem3, w_mem, km_gain, ones)
    return (_trunk(x_prompt, kvm_prompt, params), _trunk(x_sample, kvm_sample, params))
```

```python
import functools

import numpy as np
import jax
import jax.numpy as jnp
from jax import lax
from jax.experimental import pallas as pl
from jax.experimental.pallas import tpu as pltpu

D_MODEL = 1024
DEPTH = 2
HEAD_DIM = 64
A_Q_HEADS = 8
A_KV_HEADS = 2
WINDOW = 128
A_BLOCK = 128
B_HEADS = 4
GRID_W = 64
NB_ROWS = 8
NB_COLS = 16
M_HEADS = 4
N_MEM = 256
D_FF = 4 * D_MODEL
EPS = 1e-6

A_Q = A_Q_HEADS * HEAD_DIM
A_KV = A_KV_HEADS * HEAD_DIM
B_W = B_HEADS * HEAD_DIM
M_W = M_HEADS * HEAD_DIM

Q_W = A_Q + B_W + M_W
NORMED_W = Q_W + 2 * 256
PROJ_W = NORMED_W + 2 * 256
LANE_TILE = 256
KV_W = 512

NAT_HALO = NB_ROWS * GRID_W
MASKED = 1e30
LOG2E = float(np.log2(np.e))

TOKENS_PER_TILE = 512
FF_CHUNK = 1024
VMEM_LIMIT_BYTES = 48 * 1024 * 1024

F32 = jnp.float32
BF16 = jnp.bfloat16


def _dot(a, b):
    return jnp.dot(a, b, preferred_element_type=F32)


def _dot_nt(a, b):
    return lax.dot_general(a, b, (((1,), (1,)), ((), ())), preferred_element_type=F32)


def _rms_scale(x):
    return x * lax.rsqrt(jnp.mean(x * x, axis=-1, keepdims=True) + EPS)


def _head_norm(p, ones, gain):
    ss = _dot((p * p).astype(BF16), ones)
    return p * lax.rsqrt(ss * (1.0 / HEAD_DIM) + EPS) * gain


def _lane_mask(width, lo, hi, dtype):
    lane = lax.broadcasted_iota(jnp.int32, (1, width), 1)
    return jnp.where((lane >= lo) & (lane < hi), 1.0, 0.0).astype(dtype)


def _mem_kv_kernel(mem_ref, g_ref, w_ref, gain_ref, ones_ref, out_ref):
    h = (_rms_scale(mem_ref[...]) * g_ref[...]).astype(BF16)
    mkv = _dot(h, w_ref[...])
    out_ref[:, 0:M_W] = _head_norm(mkv[:, 0:M_W], ones_ref[...], gain_ref[...]).astype(BF16)
    out_ref[:, M_W:2 * M_W] = mkv[:, M_W:2 * M_W].astype(BF16)


def _mem_kv(mem, g_mem, w_mem_kv, km_gain, ones):
    batch = mem.shape[0]
    return pl.pallas_call(
        _mem_kv_kernel,
        grid=(DEPTH, batch),
        in_specs=[
            pl.BlockSpec((None, N_MEM, D_MODEL), lambda l, b: (b, 0, 0)),
            pl.BlockSpec((None, 1, D_MODEL), lambda l, b: (l, 0, 0)),
            pl.BlockSpec((None, D_MODEL, 2 * M_W), lambda l, b: (l, 0, 0)),
            pl.BlockSpec((None, 1, M_W), lambda l, b: (l, 0, 0)),
            pl.BlockSpec((LANE_TILE, LANE_TILE), lambda l, b: (0, 0)),
        ],
        out_specs=pl.BlockSpec((None, None, N_MEM, 2 * M_W), lambda l, b: (l, b, 0, 0)),
        out_shape=jax.ShapeDtypeStruct((DEPTH, batch, N_MEM, 2 * M_W), BF16),
        compiler_params=pltpu.CompilerParams(dimension_semantics=("arbitrary", "arbitrary")),
        name="mem_kv",
    )(mem, g_mem, w_mem_kv, km_gain, ones)


def _in_proj_kernel(x_ref, g_ref, w_ref, gain_ref, ones_ref, q_ref, kva_ref, kvb_ref):
    h = (_rms_scale(x_ref[...]) * g_ref[...]).astype(BF16)
    proj = _dot(h, w_ref[...])
    ones = ones_ref[...]

    def normed(c):
        sl = slice(c * LANE_TILE, (c + 1) * LANE_TILE)
        return _head_norm(proj[:, sl], ones, gain_ref[:, sl]).astype(BF16)

    for c in range(Q_W // LANE_TILE):
        q_ref[:, c * LANE_TILE:(c + 1) * LANE_TILE] = normed(c)
    kva_ref[:, 0:256] = normed(4)
    kvb_ref[:, 0:256] = normed(5)
    kva_ref[:, 256:512] = proj[:, NORMED_W:NORMED_W + 256].astype(BF16)
    kvb_ref[:, 256:512] = proj[:, NORMED_W + 256:PROJ_W].astype(BF16)


def _in_proj(x2d, g_mix_l, w_in_l, gain_l, ones):
    tokens = x2d.shape[0]
    tm = TOKENS_PER_TILE
    const = lambda i: (0, 0)
    return pl.pallas_call(
        _in_proj_kernel,
        grid=(tokens // tm,),
        in_specs=[
            pl.BlockSpec((tm, D_MODEL), lambda i: (i, 0)),
            pl.BlockSpec((1, D_MODEL), const),
            pl.BlockSpec((D_MODEL, PROJ_W), const),
            pl.BlockSpec((1, NORMED_W), const),
            pl.BlockSpec((LANE_TILE, LANE_TILE), const),
        ],
        out_specs=[
            pl.BlockSpec((tm, Q_W), lambda i: (i, 0)),
            pl.BlockSpec((tm, KV_W), lambda i: (i, 0)),
            pl.BlockSpec((tm, KV_W), lambda i: (i, 0)),
        ],
        out_shape=[
            jax.ShapeDtypeStruct((tokens, Q_W), BF16),
            jax.ShapeDtypeStruct((tokens, KV_W), BF16),
            jax.ShapeDtypeStruct((tokens, KV_W), BF16),
        ],
        compiler_params=pltpu.CompilerParams(
            dimension_semantics=("arbitrary",), vmem_limit_bytes=VMEM_LIMIT_BYTES),
        name="in_proj",
    )(x2d, g_mix_l, w_in_l, gain_l, ones)


def _attn_kernel(seq, tq,
                 x_ref, q_ref, kap_ref, kac_ref, kan_ref, kbp_ref, kbc_ref, kbn_ref, kvm_ref,
                 wbias_ref, nat_ref, sink_ref, og_ref, wout_ref, out_ref,
                 kw_buf, vw_buf, kb_buf, o_buf):
    i = pl.program_id(1)
    n_blocks = seq // A_BLOCK
    n_rows = seq // GRID_W
    blocks_per_tile = tq // A_BLOCK
    rows_per_tile = tq // GRID_W

    lo_b = _lane_mask(128, 0, HEAD_DIM, BF16)
    hi_b = _lane_mask(128, HEAD_DIM, 128, BF16)
    for src, r0, n in ((kap_ref, 0, A_BLOCK), (kac_ref, A_BLOCK, tq), (kan_ref, A_BLOCK + tq, A_BLOCK)):
        rows = slice(r0, r0 + n)
        for g in range(A_KV_HEADS):
            k2 = src[:, 128 * g:128 * (g + 1)]
            v2 = src[:, 256 + 128 * g:256 + 128 * (g + 1)]
            kw_buf[rows, 256 * g:256 * g + 128] = k2 * lo_b
            kw_buf[rows, 256 * g + 128:256 * (g + 1)] = k2 * hi_b
            vw_buf[rows, 512 * g:512 * g + 128] = v2 * lo_b
            vw_buf[rows, 512 * g + 128:512 * g + 256] = jnp.broadcast_to(_lane_mask(128, 0, 1, BF16), (n, 128))
            vw_buf[rows, 512 * g + 256:512 * g + 384] = v2 * hi_b
            vw_buf[rows, 512 * g + 384:512 * (g + 1)] = jnp.broadcast_to(_lane_mask(128, 1, 2, BF16), (n, 128))
    kb_buf[0:NAT_HALO] = kbp_ref[...]
    kb_buf[NAT_HALO:NAT_HALO + tq] = kbc_ref[...]
    kb_buf[NAT_HALO + tq:2 * NAT_HALO + tq] = kbn_ref[...]

    top_rows = lax.broadcasted_iota(jnp.int32, (2 * A_BLOCK, 1), 0) < A_BLOCK
    low_lanes = lax.broadcasted_iota(jnp.int32, (1, 128), 1) < HEAD_DIM

    def window_block(j, carry):
        blk = i * blocks_per_tile + j
        variant = jnp.where(blk == 0, 1, jnp.where(blk == n_blocks - 1, 2, 0))
        q0 = pl.multiple_of(j * A_BLOCK, A_BLOCK)
        band = pl.ds(q0, 3 * A_BLOCK)
        for g in range(A_KV_HEADS):
            qg = q_ref[pl.ds(q0, A_BLOCK), 256 * g:256 * (g + 1)]
            lhs = jnp.concatenate([qg[:, 0:128], qg[:, 128:256]], axis=0)
            keys = jnp.concatenate(
                [kw_buf[band, 256 * g:256 * g + 128], kw_buf[band, 256 * g + 128:256 * (g + 1)]], axis=0)
            s = _dot_nt(lhs, keys) + wbias_ref[variant, g]
            sink_lo = jnp.where(top_rows, sink_ref[4 * g], sink_ref[4 * g + 2])
            sink_hi = jnp.where(top_rows, sink_ref[4 * g + 1], sink_ref[4 * g + 3])
            s_lo = s[:, 0:3 * A_BLOCK]
            s_hi = s[:, 3 * A_BLOCK:6 * A_BLOCK]
            m_lo = jnp.max(s_lo, axis=-1, keepdims=True)
            m_hi = jnp.max(s_hi, axis=-1, keepdims=True)
            p = jnp.concatenate([jnp.exp2(s_lo - m_lo), jnp.exp2(s_hi - m_hi)], axis=1).astype(BF16)
            vals = jnp.concatenate(
                [vw_buf[band, 512 * g:512 * g + 256], vw_buf[band, 512 * g + 256:512 * (g + 1)]], axis=0)
            o = _dot(p, vals)
            l_lo = o[:, 128:129] + jnp.exp2(sink_lo - m_lo)
            l_hi = o[:, 129:130] + jnp.exp2(sink_hi - m_hi)
            res = o[:, 0:128] / jnp.where(low_lanes, l_lo, l_hi)
            o_buf[pl.ds(q0, A_BLOCK), 256 * g:256 * g + 128] = res[0:A_BLOCK]
            o_buf[pl.ds(q0, A_BLOCK), 256 * g + 128:256 * (g + 1)] = res[A_BLOCK:2 * A_BLOCK]
        return carry

    lax.fori_loop(0, blocks_per_tile, window_block, 0, unroll=4)

    head_b = [_lane_mask(B_W, HEAD_DIM * h, HEAD_DIM * (h + 1), BF16) for h in range(B_HEADS)]
    head_f = [_lane_mask(B_W, HEAD_DIM * h, HEAD_DIM * (h + 1), F32) for h in range(B_HEADS)]

    def nat_row(rr, carry):
        r = i * rows_per_tile + rr
        start = jnp.clip(r - NB_ROWS // 2, 0, n_rows - NB_ROWS)
        d0 = (NB_ROWS - 1) - (r - start)
        k0 = pl.multiple_of((start - i * rows_per_tile + NB_ROWS) * GRID_W, GRID_W)
        q0 = pl.multiple_of(rr * GRID_W, GRID_W)
        kwin = kb_buf[pl.ds(k0, NAT_HALO), 0:B_W]
        vwin = kb_buf[pl.ds(k0, NAT_HALO), B_W:2 * B_W]
        qrow = q_ref[pl.ds(q0, GRID_W), A_Q:A_Q + B_W]
        lhs = jnp.concatenate([qrow * head_b[h] for h in range(B_HEADS)], axis=0)
        bias = jnp.concatenate(
            [jnp.concatenate([nat_ref[h, d0 + 2 * m] for m in range(NB_ROWS // 2)], axis=1)
             for h in range(B_HEADS)], axis=0)
        s = _dot_nt(lhs, kwin) + bias
        p = jnp.exp2(s - jnp.max(s, axis=-1, keepdims=True))
        rl = 1.0 / jnp.sum(p, axis=-1, keepdims=True)
        pv = _dot(p.astype(BF16), vwin) * rl
        o = pv[0:GRID_W] * head_f[0]
        for h in range(1, B_HEADS):
            o = o + pv[GRID_W * h:GRID_W * (h + 1)] * head_f[h]
        o_buf[pl.ds(q0, GRID_W), A_Q:A_Q + B_W] = o
        return carry

    lax.fori_loop(0, rows_per_tile, nat_row, 0, unroll=8)

    km = kvm_ref[:, 0:M_W]
    vm = kvm_ref[:, M_W:2 * M_W]
    qm = q_ref[:, A_Q + B_W:Q_W]
    s = _dot_nt(qm, jnp.concatenate([km * head_b[h] for h in range(M_HEADS)], axis=0))
    probs = []
    scale = jnp.zeros((tq, M_W), F32)
    for h in range(M_HEADS):
        s_h = s[:, N_MEM * h:N_MEM * (h + 1)]
        p = jnp.exp2(s_h - jnp.max(s_h, axis=-1, keepdims=True))
        scale = scale + head_f[h] * (1.0 / jnp.sum(p, axis=-1, keepdims=True))
        probs.append(p.astype(BF16))
    om = _dot(jnp.concatenate(probs, axis=1),
              jnp.concatenate([vm * head_b[h] for h in range(M_HEADS)], axis=0))
    o_buf[:, A_Q + B_W:Q_W] = om * scale

    o = jnp.concatenate(
        [_rms_scale(o_buf[:, 0:A_Q]), _rms_scale(o_buf[:, A_Q:A_Q + B_W]), _rms_scale(o_buf[:, A_Q + B_W:Q_W])],
        axis=-1) * og_ref[...]
    out_ref[...] = x_ref[...] + _dot(o.astype(BF16), wout_ref[...])


def _attention(x, q, kva, kvb, kvm_l, wbias_tab, nat_tab, sink_l, og_l, wout_l):
    batch, seq, _ = x.shape
    tq = TOKENS_PER_TILE
    assert seq % tq == 0 and tq % NAT_HALO == 0 and seq >= 2 * A_BLOCK and seq >= NB_ROWS * GRID_W
    nt = seq // tq
    a_per_tile = tq // A_BLOCK
    n_ablocks = seq // A_BLOCK
    b_per_tile = tq // NAT_HALO
    n_bblocks = seq // NAT_HALO
    const2 = lambda b, i: (0, 0)
    const4 = lambda b, i: (0, 0, 0, 0)
    once = pl.Buffered(1)
    return pl.pallas_call(
        functools.partial(_attn_kernel, seq, tq),
        grid=(batch, nt),
        in_specs=[
            pl.BlockSpec((None, tq, D_MODEL), lambda b, i: (b, i, 0)),
            pl.BlockSpec((None, tq, Q_W), lambda b, i: (b, i, 0)),
            pl.BlockSpec((None, A_BLOCK, KV_W), lambda b, i: (b, jnp.maximum(i * a_per_tile - 1, 0), 0)),
            pl.BlockSpec((None, tq, KV_W), lambda b, i: (b, i, 0)),
            pl.BlockSpec((None, A_BLOCK, KV_W), lambda b, i: (b, jnp.minimum((i + 1) * a_per_tile, n_ablocks - 1), 0)),
            pl.BlockSpec((None, NAT_HALO, KV_W), lambda b, i: (b, jnp.maximum(i * b_per_tile - 1, 0), 0)),
            pl.BlockSpec((None, tq, KV_W), lambda b, i: (b, i, 0)),
            pl.BlockSpec((None, NAT_HALO, KV_W), lambda b, i: (b, jnp.minimum((i + 1) * b_per_tile, n_bblocks - 1), 0)),
            pl.BlockSpec((None, N_MEM, 2 * M_W), lambda b, i: (b, 0, 0)),
            pl.BlockSpec((3, A_KV_HEADS, 2 * A_BLOCK, 6 * A_BLOCK), const4, pipeline_mode=once),
            pl.BlockSpec((B_HEADS, 2 * NB_ROWS - 2, GRID_W, 2 * GRID_W), const4, pipeline_mode=once),
            pl.BlockSpec(memory_space=pltpu.SMEM),
            pl.BlockSpec((1, Q_W), const2),
            pl.BlockSpec((Q_W, D_MODEL), const2, pipeline_mode=once),
        ],
        out_specs=pl.BlockSpec((None, tq, D_MODEL), lambda b, i: (b, i, 0)),
        out_shape=jax.ShapeDtypeStruct((batch, seq, D_MODEL), F32),
        scratch_shapes=[
            pltpu.VMEM((tq + 2 * A_BLOCK, 2 * 256), BF16),
            pltpu.VMEM((tq + 2 * A_BLOCK, 2 * 512), BF16),
            pltpu.VMEM((tq + 2 * NAT_HALO, KV_W), BF16),
            pltpu.VMEM((tq, Q_W), F32),
        ],
        compiler_params=pltpu.CompilerParams(
            dimension_semantics=("arbitrary", "arbitrary"), vmem_limit_bytes=VMEM_LIMIT_BYTES),
        name="attention",
    )(x, q, kva, kva, kva, kvb, kvb, kvb, kvm_l, wbias_tab, nat_tab, sink_l, og_l, wout_l)


def _ffn_kernel(x_ref, g_ref, w1_ref, w2_ref, out_ref):
    x = x_ref[...]
    h = (_rms_scale(x) * g_ref[...]).astype(BF16)
    acc = x
    for c in range(D_FF // FF_CHUNK):
        f = _dot(h, w1_ref[:, c * FF_CHUNK:(c + 1) * FF_CHUNK])
        a = jnp.square(jnp.maximum(f, 0.0)).astype(BF16)
        acc = acc + _dot(a, w2_ref[c * FF_CHUNK:(c + 1) * FF_CHUNK, :])
    out_ref[...] = acc


def _ffn(x2d, g_ff_l, w1_l, w2_l):
    tokens = x2d.shape[0]
    tm = TOKENS_PER_TILE
    const = lambda i: (0, 0)
    return pl.pallas_call(
        _ffn_kernel,
        grid=(tokens // tm,),
        in_specs=[
            pl.BlockSpec((tm, D_MODEL), lambda i: (i, 0)),
            pl.BlockSpec((1, D_MODEL), const),
            pl.BlockSpec((D_MODEL, D_FF), const, pipeline_mode=pl.Buffered(1)),
            pl.BlockSpec((D_FF, D_MODEL), const, pipeline_mode=pl.Buffered(1)),
        ],
        out_specs=pl.BlockSpec((tm, D_MODEL), lambda i: (i, 0)),
        out_shape=jax.ShapeDtypeStruct((tokens, D_MODEL), F32),
        compiler_params=pltpu.CompilerParams(
            dimension_semantics=("arbitrary",), vmem_limit_bytes=VMEM_LIMIT_BYTES),
        name="ffn",
    )(x2d, g_ff_l, w1_l, w2_l)


def _window_bias_table():
    qi = np.arange(A_BLOCK)[:, None]
    si = np.arange(3 * A_BLOCK)[None, :]
    dist = np.abs(si - A_BLOCK - qi).astype(np.float32)
    band = dist <= WINDOW
    keep = [band, band & (si >= A_BLOCK), band & (si < 2 * A_BLOCK)]
    slopes = (2.0 ** (-8.0 * np.arange(1, A_Q_HEADS + 1) / A_Q_HEADS)).astype(np.float32)
    tab = np.empty((3, A_KV_HEADS, 2 * A_BLOCK, 6 * A_BLOCK), np.float32)
    for v in range(3):
        for g in range(A_KV_HEADS):
            for hh in range(4):
                rows = slice(A_BLOCK * (hh // 2), A_BLOCK * (hh // 2 + 1))
                cols = slice(3 * A_BLOCK * (hh % 2), 3 * A_BLOCK * (hh % 2 + 1))
                tab[v, g, rows, cols] = np.where(keep[v], -(slopes[4 * g + hh] * dist) * LOG2E, -MASKED)
    return jnp.asarray(tab)


def _neighbourhood_bias_table(rpb_l):
    c = np.arange(GRID_W)
    dc = np.clip(c[None, :] - c[:, None] + (NB_COLS - 1), 0, 2 * NB_COLS - 2)
    onehot = jnp.asarray((dc[None] == np.arange(2 * NB_COLS - 1)[:, None, None]).astype(np.float32))
    t = jnp.einsum("hdc,cqk->hdqk", rpb_l.astype(F32), onehot, precision=lax.Precision.HIGHEST) * LOG2E
    cstart = np.clip(c - NB_COLS // 2, 0, GRID_W - NB_COLS)
    cvalid = (c[None, :] >= cstart[:, None]) & (c[None, :] < cstart[:, None] + NB_COLS)
    t = jnp.where(cvalid[None, None], t, -MASKED)
    return jnp.concatenate([t[:, :-1], t[:, 1:]], axis=-1)


def _pack_in_proj(w_in_l):
    qa, ka, va, qb, kb, vb, qm = jnp.split(
        w_in_l, [A_Q, A_Q + A_KV, A_Q + 2 * A_KV, A_Q + 2 * A_KV + B_W,
                 A_Q + 2 * A_KV + 2 * B_W, A_Q + 2 * A_KV + 3 * B_W], axis=1)
    twice = lambda t: jnp.concatenate(
        [t[:, :HEAD_DIM], t[:, :HEAD_DIM], t[:, HEAD_DIM:], t[:, HEAD_DIM:]], axis=1)
    return jnp.concatenate([qa, qb, qm, twice(ka), kb, twice(va), vb], axis=1).astype(BF16)


def _pack_qk_gain(qk_gain_l):
    scale = HEAD_DIM ** -0.5 * LOG2E
    g = qk_gain_l.astype(F32)
    return jnp.concatenate([
        jnp.tile(g[0], A_Q_HEADS) * scale, jnp.tile(g[2], B_HEADS) * scale, jnp.tile(g[4], M_HEADS) * scale,
        jnp.tile(g[1], 2 * A_KV_HEADS), jnp.tile(g[3], B_HEADS)])[None, :]


def _trunk(x, kvm, params):
    batch, seq, _ = x.shape
    for l in range(DEPTH):
        p = params[l]
        q, kva, kvb = _in_proj(x.reshape(batch * seq, D_MODEL), p["g_mix"], p["w_in"], p["qk_gain"], params["ones"])
        x = _attention(x, q.reshape(batch, seq, Q_W), kva.reshape(batch, seq, KV_W), kvb.reshape(batch, seq, KV_W),
                       kvm[l], params["wbias"], p["nat"], p["sink"], p["o_gain"], p["w_out"])
        x = _ffn(x.reshape(batch * seq, D_MODEL), p["g_ff"], p["w_ff1"], p["w_ff2"]).reshape(batch, seq, D_MODEL)
    return x


def kernel(x_prompt, x_sample, mem_prompt, mem_sample, g_mix, w_in, qk_gain, sink, rpb,
           o_gain, w_out, g_mem, w_mem_kv, g_ff, w_ff1, w_ff2):
    ones = jnp.asarray(np.kron(np.eye(LANE_TILE // HEAD_DIM), np.ones((HEAD_DIM, HEAD_DIM))), BF16)
    params = {"ones": ones, "wbias": _window_bias_table()}
    for l in range(DEPTH):
        params[l] = {
            "g_mix": g_mix[l].astype(F32)[None, :],
            "w_in": _pack_in_proj(w_in[l]),
            "qk_gain": _pack_qk_gain(qk_gain[l]),
            "nat": _neighbourhood_bias_table(rpb[l]),
            "sink": sink[l].astype(F32) * LOG2E,
            "o_gain": o_gain[l].astype(F32)[None, :],
            "w_out": w_out[l].astype(BF16),
            "g_ff": g_ff[l].astype(F32)[None, :],
            "w_ff1": w_ff1[l].astype(BF16),
            "w_ff2": w_ff2[l].astype(BF16),
        }
    km_gain = jnp.tile(qk_gain[:, 5].astype(F32), (1, M_HEADS))[:, None, :]
    g_mem3 = g_mem.astype(F32)[:, None, :]
    w_mem = w_mem_kv.astype(BF16)
    kvm_prompt = _mem_kv(mem_prompt, g_mem3, w_mem, km_gain, ones)
    kvm_sample = _mem_kv(mem_sample, g_mem3, w_mem, km_gain, ones)
    return (_trunk(x_prompt, kvm_prompt, params), _trunk(x_sample, kvm_sample, params))
```

```python
import functools

import numpy as np
import jax
import jax.numpy as jnp
from jax import lax
from jax.experimental import pallas as pl
from jax.experimental.pallas import tpu as pltpu

D_MODEL = 1024
DEPTH = 2
HEAD_DIM = 64
A_Q_HEADS = 8
A_KV_HEADS = 2
WINDOW = 128
A_BLOCK = 128
B_HEADS = 4
GRID_W = 64
NB_ROWS = 8
NB_COLS = 16
M_HEADS = 4
N_MEM = 256
D_FF = 4 * D_MODEL
EPS = 1e-6

A_Q = A_Q_HEADS * HEAD_DIM
A_KV = A_KV_HEADS * HEAD_DIM
B_W = B_HEADS * HEAD_DIM
M_W = M_HEADS * HEAD_DIM

Q_W = A_Q + B_W + M_W
NORMED_W = Q_W + 2 * 256
PROJ_W = NORMED_W + 2 * 256
LANE_TILE = 256
KV_W = 512

NAT_HALO = NB_ROWS * GRID_W
MASKED = 1e30
LOG2E = float(np.log2(np.e))

PROJ_TILE = 1024
ATTN_TILE = 1024
FFN_TILE = 512
FF_CHUNK = 1024
VMEM_LIMIT_BYTES = 58 * 1024 * 1024

F32 = jnp.float32
BF16 = jnp.bfloat16


def _dot(a, b):
    return jnp.dot(a, b, preferred_element_type=F32)


def _dot_nt(a, b):
    return lax.dot_general(a, b, (((1,), (1,)), ((), ())), preferred_element_type=F32)


def _rms_scale(x):
    return x * lax.rsqrt(jnp.mean(x * x, axis=-1, keepdims=True) + EPS)


def _head_norm(p, ones, gain):
    ss = _dot((p * p).astype(BF16), ones)
    return p * lax.rsqrt(ss * (1.0 / HEAD_DIM) + EPS) * gain


def _lane_mask(width, lo, hi, dtype):
    lane = lax.broadcasted_iota(jnp.int32, (1, width), 1)
    return jnp.where((lane >= lo) & (lane < hi), 1.0, 0.0).astype(dtype)


def _mem_kv_kernel(mem_ref, g_ref, w_ref, gain_ref, ones_ref, out_ref):
    h = (_rms_scale(mem_ref[...]) * g_ref[...]).astype(BF16)
    mkv = _dot(h, w_ref[...])
    out_ref[:, 0:M_W] = _head_norm(mkv[:, 0:M_W], ones_ref[...], gain_ref[...]).astype(BF16)
    out_ref[:, M_W:2 * M_W] = mkv[:, M_W:2 * M_W].astype(BF16)


def _mem_kv(mem, g_mem, w_mem_kv, km_gain, ones):
    batch = mem.shape[0]
    return pl.pallas_call(
        _mem_kv_kernel,
        grid=(DEPTH, batch),
        in_specs=[
            pl.BlockSpec((None, N_MEM, D_MODEL), lambda l, b: (b, 0, 0)),
            pl.BlockSpec((None, 1, D_MODEL), lambda l, b: (l, 0, 0)),
            pl.BlockSpec((None, D_MODEL, 2 * M_W), lambda l, b: (l, 0, 0)),
            pl.BlockSpec((None, 1, M_W), lambda l, b: (l, 0, 0)),
            pl.BlockSpec((LANE_TILE, LANE_TILE), lambda l, b: (0, 0)),
        ],
        out_specs=pl.BlockSpec((None, None, N_MEM, 2 * M_W), lambda l, b: (l, b, 0, 0)),
        out_shape=jax.ShapeDtypeStruct((DEPTH, batch, N_MEM, 2 * M_W), BF16),
        compiler_params=pltpu.CompilerParams(dimension_semantics=("arbitrary", "arbitrary")),
        name="mem_kv",
    )(mem, g_mem, w_mem_kv, km_gain, ones)


def _in_proj_kernel(x_ref, g_ref, w_ref, gain_ref, ones_ref, q_ref, kva_ref, kvb_ref):
    h = (_rms_scale(x_ref[...]) * g_ref[...]).astype(BF16)
    ones = ones_ref[...]
    outs = (q_ref, q_ref, q_ref, q_ref, kva_ref, kvb_ref, kva_ref, kvb_ref)
    offs = (0, 256, 512, 768, 0, 0, 256, 256)
    for pair in range(PROJ_W // (2 * LANE_TILE)):
        proj = _dot(h, w_ref[:, 2 * pair * LANE_TILE:2 * (pair + 1) * LANE_TILE])
        for half in range(2):
            c = 2 * pair + half
            chunk = proj[:, half * LANE_TILE:(half + 1) * LANE_TILE]
            if c * LANE_TILE < NORMED_W:
                chunk = _head_norm(chunk, ones, gain_ref[:, c * LANE_TILE:(c + 1) * LANE_TILE])
            outs[c][:, offs[c]:offs[c] + LANE_TILE] = chunk.astype(BF16)


def _in_proj(x2d, g_mix_l, w_in_l, gain_l, ones):
    tokens = x2d.shape[0]
    tm = PROJ_TILE
    assert tokens % tm == 0
    const = lambda i: (0, 0)
    return pl.pallas_call(
        _in_proj_kernel,
        grid=(tokens // tm,),
        in_specs=[
            pl.BlockSpec((tm, D_MODEL), lambda i: (i, 0)),
            pl.BlockSpec((1, D_MODEL), const),
            pl.BlockSpec((D_MODEL, PROJ_W), const),
            pl.BlockSpec((1, NORMED_W), const),
            pl.BlockSpec((LANE_TILE, LANE_TILE), const),
        ],
        out_specs=[
            pl.BlockSpec((tm, Q_W), lambda i: (i, 0)),
            pl.BlockSpec((tm, KV_W), lambda i: (i, 0)),
            pl.BlockSpec((tm, KV_W), lambda i: (i, 0)),
        ],
        out_shape=[
            jax.ShapeDtypeStruct((tokens, Q_W), BF16),
            jax.ShapeDtypeStruct((tokens, KV_W), BF16),
            jax.ShapeDtypeStruct((tokens, KV_W), BF16),
        ],
        compiler_params=pltpu.CompilerParams(
            dimension_semantics=("arbitrary",), vmem_limit_bytes=VMEM_LIMIT_BYTES),
        name="in_proj",
    )(x2d, g_mix_l, w_in_l, gain_l, ones)


def _attn_kernel(seq, tq,
                 x_ref, q_ref, kap_ref, kac_ref, kan_ref, kbp_ref, kbc_ref, kbn_ref, kvm_ref,
                 wbias_ref, nat_ref, sink_ref, og_ref, wout_ref, out_ref,
                 kw_buf, vw_buf, kb_buf, o_buf):
    i = pl.program_id(1)
    n_blocks = seq // A_BLOCK
    n_rows = seq // GRID_W
    blocks_per_tile = tq // A_BLOCK
    rows_per_tile = tq // GRID_W

    lo_b = _lane_mask(128, 0, HEAD_DIM, BF16)
    hi_b = _lane_mask(128, HEAD_DIM, 128, BF16)
    for src, r0, n in ((kap_ref, 0, A_BLOCK), (kac_ref, A_BLOCK, tq), (kan_ref, A_BLOCK + tq, A_BLOCK)):
        rows = slice(r0, r0 + n)
        for g in range(A_KV_HEADS):
            k2 = src[:, 128 * g:128 * (g + 1)]
            v2 = src[:, 256 + 128 * g:256 + 128 * (g + 1)]
            kw_buf[rows, 256 * g:256 * g + 128] = k2 * lo_b
            kw_buf[rows, 256 * g + 128:256 * (g + 1)] = k2 * hi_b
            vw_buf[rows, 512 * g:512 * g + 128] = v2 * lo_b
            vw_buf[rows, 512 * g + 128:512 * g + 256] = jnp.broadcast_to(_lane_mask(128, 0, 1, BF16), (n, 128))
            vw_buf[rows, 512 * g + 256:512 * g + 384] = v2 * hi_b
            vw_buf[rows, 512 * g + 384:512 * (g + 1)] = jnp.broadcast_to(_lane_mask(128, 1, 2, BF16), (n, 128))
    kb_buf[0:NAT_HALO] = kbp_ref[...]
    kb_buf[NAT_HALO:NAT_HALO + tq] = kbc_ref[...]
    kb_buf[NAT_HALO + tq:2 * NAT_HALO + tq] = kbn_ref[...]

    top_rows = lax.broadcasted_iota(jnp.int32, (2 * A_BLOCK, 1), 0) < A_BLOCK
    low_lanes = lax.broadcasted_iota(jnp.int32, (1, 128), 1) < HEAD_DIM

    def window_block(j, carry):
        blk = i * blocks_per_tile + j
        variant = jnp.where(blk == 0, 1, jnp.where(blk == n_blocks - 1, 2, 0))
        q0 = pl.multiple_of(j * A_BLOCK, A_BLOCK)
        band = pl.ds(q0, 3 * A_BLOCK)
        for g in range(A_KV_HEADS):
            qg = q_ref[pl.ds(q0, A_BLOCK), 256 * g:256 * (g + 1)]
            lhs = jnp.concatenate([qg[:, 0:128], qg[:, 128:256]], axis=0)
            keys = jnp.concatenate(
                [kw_buf[band, 256 * g:256 * g + 128], kw_buf[band, 256 * g + 128:256 * (g + 1)]], axis=0)
            s = _dot_nt(lhs, keys) + wbias_ref[variant, g]
            sink_lo = jnp.where(top_rows, sink_ref[4 * g], sink_ref[4 * g + 2])
            sink_hi = jnp.where(top_rows, sink_ref[4 * g + 1], sink_ref[4 * g + 3])
            s_lo = s[:, 0:3 * A_BLOCK]
            s_hi = s[:, 3 * A_BLOCK:6 * A_BLOCK]
            m_lo = jnp.max(s_lo, axis=-1, keepdims=True)
            m_hi = jnp.max(s_hi, axis=-1, keepdims=True)
            p = jnp.concatenate([jnp.exp2(s_lo - m_lo), jnp.exp2(s_hi - m_hi)], axis=1).astype(BF16)
            vals = jnp.concatenate(
                [vw_buf[band, 512 * g:512 * g + 256], vw_buf[band, 512 * g + 256:512 * (g + 1)]], axis=0)
            o = _dot(p, vals)
            l_lo = o[:, 128:129] + jnp.exp2(sink_lo - m_lo)
            l_hi = o[:, 129:130] + jnp.exp2(sink_hi - m_hi)
            res = o[:, 0:128] / jnp.where(low_lanes, l_lo, l_hi)
            o_buf[pl.ds(q0, A_BLOCK), 256 * g:256 * g + 128] = res[0:A_BLOCK]
            o_buf[pl.ds(q0, A_BLOCK), 256 * g + 128:256 * (g + 1)] = res[A_BLOCK:2 * A_BLOCK]
        return carry

    lax.fori_loop(0, blocks_per_tile, window_block, 0, unroll=blocks_per_tile)

    head_b = [_lane_mask(B_W, HEAD_DIM * h, HEAD_DIM * (h + 1), BF16) for h in range(B_HEADS)]
    head_f = [_lane_mask(B_W, HEAD_DIM * h, HEAD_DIM * (h + 1), F32) for h in range(B_HEADS)]

    def nat_scores(rr):
        r = i * rows_per_tile + rr
        start = jnp.clip(r - NB_ROWS // 2, 0, n_rows - NB_ROWS)
        d0 = (NB_ROWS - 1) - (r - start)
        k0 = pl.multiple_of((start - i * rows_per_tile + NB_ROWS) * GRID_W, GRID_W)
        kwin = kb_buf[pl.ds(k0, NAT_HALO), 0:B_W]
        qrow = q_ref[rr * GRID_W:(rr + 1) * GRID_W, A_Q:A_Q + B_W]
        lhs = jnp.concatenate([qrow * head_b[h] for h in range(B_HEADS)], axis=0)
        bias = jnp.concatenate(
            [jnp.concatenate([nat_ref[h, d0 + 2 * m] for m in range(NB_ROWS // 2)], axis=1)
             for h in range(B_HEADS)], axis=0)
        return _dot_nt(lhs, kwin) + bias, k0

    def nat_finish(rr, s, k0):
        vwin = kb_buf[pl.ds(k0, NAT_HALO), B_W:2 * B_W]
        p = jnp.exp2(s - jnp.max(s, axis=-1, keepdims=True))
        rl = 1.0 / jnp.sum(p, axis=-1, keepdims=True)
        pv = _dot(p.astype(BF16), vwin) * rl
        o = pv[0:GRID_W] * head_f[0]
        for h in range(1, B_HEADS):
            o = o + pv[GRID_W * h:GRID_W * (h + 1)] * head_f[h]
        o_buf[rr * GRID_W:(rr + 1) * GRID_W, A_Q:A_Q + B_W] = o

    pending = nat_scores(0)
    for rr in range(1, rows_per_tile):
        ahead = nat_scores(rr)
        nat_finish(rr - 1, *pending)
        pending = ahead
    nat_finish(rows_per_tile - 1, *pending)

    km = kvm_ref[:, 0:M_W]
    vm = kvm_ref[:, M_W:2 * M_W]
    qm = q_ref[:, A_Q + B_W:Q_W]
    s = _dot_nt(qm, jnp.concatenate([km * head_b[h] for h in range(M_HEADS)], axis=0))
    probs = []
    scale = jnp.zeros((tq, M_W), F32)
    for h in range(M_HEADS):
        s_h = s[:, N_MEM * h:N_MEM * (h + 1)]
        p = jnp.exp2(s_h - jnp.max(s_h, axis=-1, keepdims=True))
        scale = scale + head_f[h] * (1.0 / jnp.sum(p, axis=-1, keepdims=True))
        probs.append(p.astype(BF16))
    om = _dot(jnp.concatenate(probs, axis=1),
              jnp.concatenate([vm * head_b[h] for h in range(M_HEADS)], axis=0))
    o_buf[:, A_Q + B_W:Q_W] = om * scale

    o = jnp.concatenate(
        [_rms_scale(o_buf[:, 0:A_Q]), _rms_scale(o_buf[:, A_Q:A_Q + B_W]), _rms_scale(o_buf[:, A_Q + B_W:Q_W])],
        axis=-1) * og_ref[...]
    out_ref[...] = x_ref[...] + _dot(o.astype(BF16), wout_ref[...])


def _attention(x, q, kva, kvb, kvm_l, wbias_tab, nat_tab, sink_l, og_l, wout_l):
    batch, seq, _ = x.shape
    tq = ATTN_TILE
    assert seq % tq == 0 and tq % NAT_HALO == 0 and seq >= 2 * A_BLOCK and seq >= NB_ROWS * GRID_W
    nt = seq // tq
    a_per_tile = tq // A_BLOCK
    n_ablocks = seq // A_BLOCK
    b_per_tile = tq // NAT_HALO
    n_bblocks = seq // NAT_HALO
    const2 = lambda b, i: (0, 0)
    const4 = lambda b, i: (0, 0, 0, 0)
    once = pl.Buffered(1)
    return pl.pallas_call(
        functools.partial(_attn_kernel, seq, tq),
        grid=(batch, nt),
        in_specs=[
            pl.BlockSpec((None, tq, D_MODEL), lambda b, i: (b, i, 0)),
            pl.BlockSpec((None, tq, Q_W), lambda b, i: (b, i, 0)),
            pl.BlockSpec((None, A_BLOCK, KV_W), lambda b, i: (b, jnp.maximum(i * a_per_tile - 1, 0), 0)),
            pl.BlockSpec((None, tq, KV_W), lambda b, i: (b, i, 0)),
            pl.BlockSpec((None, A_BLOCK, KV_W), lambda b, i: (b, jnp.minimum((i + 1) * a_per_tile, n_ablocks - 1), 0)),
            pl.BlockSpec((None, NAT_HALO, KV_W), lambda b, i: (b, jnp.maximum(i * b_per_tile - 1, 0), 0)),
            pl.BlockSpec((None, tq, KV_W), lambda b, i: (b, i, 0)),
            pl.BlockSpec((None, NAT_HALO, KV_W), lambda b, i: (b, jnp.minimum((i + 1) * b_per_tile, n_bblocks - 1), 0)),
            pl.BlockSpec((None, N_MEM, 2 * M_W), lambda b, i: (b, 0, 0)),
            pl.BlockSpec((3, A_KV_HEADS, 2 * A_BLOCK, 6 * A_BLOCK), const4, pipeline_mode=once),
            pl.BlockSpec((B_HEADS, 2 * NB_ROWS - 2, GRID_W, 2 * GRID_W), const4, pipeline_mode=once),
            pl.BlockSpec(memory_space=pltpu.SMEM),
            pl.BlockSpec((1, Q_W), const2),
            pl.BlockSpec((Q_W, D_MODEL), const2, pipeline_mode=once),
        ],
        out_specs=pl.BlockSpec((None, tq, D_MODEL), lambda b, i: (b, i, 0)),
        out_shape=jax.ShapeDtypeStruct((batch, seq, D_MODEL), F32),
        scratch_shapes=[
            pltpu.VMEM((tq + 2 * A_BLOCK, 2 * 256), BF16),
            pltpu.VMEM((tq + 2 * A_BLOCK, 2 * 512), BF16),
            pltpu.VMEM((tq + 2 * NAT_HALO, KV_W), BF16),
            pltpu.VMEM((tq, Q_W), F32),
        ],
        compiler_params=pltpu.CompilerParams(
            dimension_semantics=("arbitrary", "arbitrary"), vmem_limit_bytes=VMEM_LIMIT_BYTES),
        name="attention",
    )(x, q, kva, kva, kva, kvb, kvb, kvb, kvm_l, wbias_tab, nat_tab, sink_l, og_l, wout_l)


def _ffn_kernel(x_ref, g_ref, w1_ref, w2_ref, out_ref):
    x = x_ref[...]
    h = (_rms_scale(x) * g_ref[...]).astype(BF16)
    acc = x
    for c in range(D_FF // FF_CHUNK):
        f = _dot(h, w1_ref[:, c * FF_CHUNK:(c + 1) * FF_CHUNK])
        a = jnp.square(jnp.maximum(f, 0.0)).astype(BF16)
        acc = acc + _dot(a, w2_ref[c * FF_CHUNK:(c + 1) * FF_CHUNK, :])
    out_ref[...] = acc


def _ffn(x2d, g_ff_l, w1_l, w2_l):
    tokens = x2d.shape[0]
    tm = FFN_TILE
    assert tokens % tm == 0
    const = lambda i: (0, 0)
    return pl.pallas_call(
        _ffn_kernel,
        grid=(tokens // tm,),
        in_specs=[
            pl.BlockSpec((tm, D_MODEL), lambda i: (i, 0)),
            pl.BlockSpec((1, D_MODEL), const),
            pl.BlockSpec((D_MODEL, D_FF), const, pipeline_mode=pl.Buffered(1)),
            pl.BlockSpec((D_FF, D_MODEL), const, pipeline_mode=pl.Buffered(1)),
        ],
        out_specs=pl.BlockSpec((tm, D_MODEL), lambda i: (i, 0)),
        out_shape=jax.ShapeDtypeStruct((tokens, D_MODEL), F32),
        compiler_params=pltpu.CompilerParams(
            dimension_semantics=("arbitrary",), vmem_limit_bytes=VMEM_LIMIT_BYTES),
        name="ffn",
    )(x2d, g_ff_l, w1_l, w2_l)


def _window_bias_table():
    qi = np.arange(A_BLOCK)[:, None]
    si = np.arange(3 * A_BLOCK)[None, :]
    dist = np.abs(si - A_BLOCK - qi).astype(np.float32)
    band = dist <= WINDOW
    keep = [band, band & (si >= A_BLOCK), band & (si < 2 * A_BLOCK)]
    slopes = (2.0 ** (-8.0 * np.arange(1, A_Q_HEADS + 1) / A_Q_HEADS)).astype(np.float32)
    tab = np.empty((3, A_KV_HEADS, 2 * A_BLOCK, 6 * A_BLOCK), np.float32)
    for v in range(3):
        for g in range(A_KV_HEADS):
            for hh in range(4):
                rows = slice(A_BLOCK * (hh // 2), A_BLOCK * (hh // 2 + 1))
                cols = slice(3 * A_BLOCK * (hh % 2), 3 * A_BLOCK * (hh % 2 + 1))
                tab[v, g, rows, cols] = np.where(keep[v], -(slopes[4 * g + hh] * dist) * LOG2E, -MASKED)
    return jnp.asarray(tab)


def _neighbourhood_bias_table(rpb_l):
    c = np.arange(GRID_W)
    dc = np.clip(c[None, :] - c[:, None] + (NB_COLS - 1), 0, 2 * NB_COLS - 2)
    onehot = jnp.asarray((dc[None] == np.arange(2 * NB_COLS - 1)[:, None, None]).astype(np.float32))
    t = jnp.einsum("hdc,cqk->hdqk", rpb_l.astype(F32), onehot, precision=lax.Precision.HIGHEST) * LOG2E
    cstart = np.clip(c - NB_COLS // 2, 0, GRID_W - NB_COLS)
    cvalid = (c[None, :] >= cstart[:, None]) & (c[None, :] < cstart[:, None] + NB_COLS)
    t = jnp.where(cvalid[None, None], t, -MASKED)
    return jnp.concatenate([t[:, :-1], t[:, 1:]], axis=-1)


def _pack_in_proj(w_in_l):
    qa, ka, va, qb, kb, vb, qm = jnp.split(
        w_in_l, [A_Q, A_Q + A_KV, A_Q + 2 * A_KV, A_Q + 2 * A_KV + B_W,
                 A_Q + 2 * A_KV + 2 * B_W, A_Q + 2 * A_KV + 3 * B_W], axis=1)
    twice = lambda t: jnp.concatenate(
        [t[:, :HEAD_DIM], t[:, :HEAD_DIM], t[:, HEAD_DIM:], t[:, HEAD_DIM:]], axis=1)
    return jnp.concatenate([qa, qb, qm, twice(ka), kb, twice(va), vb], axis=1).astype(BF16)


def _pack_qk_gain(qk_gain_l):
    scale = HEAD_DIM ** -0.5 * LOG2E
    g = qk_gain_l.astype(F32)
    return jnp.concatenate([
        jnp.tile(g[0], A_Q_HEADS) * scale, jnp.tile(g[2], B_HEADS) * scale, jnp.tile(g[4], M_HEADS) * scale,
        jnp.tile(g[1], 2 * A_KV_HEADS), jnp.tile(g[3], B_HEADS)])[None, :]


def _trunk(x, kvm, params):
    batch, seq, _ = x.shape
    for l in range(DEPTH):
        p = params[l]
        q, kva, kvb = _in_proj(x.reshape(batch * seq, D_MODEL), p["g_mix"], p["w_in"], p["qk_gain"], params["ones"])
        x = _attention(x, q.reshape(batch, seq, Q_W), kva.reshape(batch, seq, KV_W), kvb.reshape(batch, seq, KV_W),
                       kvm[l], params["wbias"], p["nat"], p["sink"], p["o_gain"], p["w_out"])
        x = _ffn(x.reshape(batch * seq, D_MODEL), p["g_ff"], p["w_ff1"], p["w_ff2"]).reshape(batch, seq, D_MODEL)
    return x


def kernel(x_prompt, x_sample, mem_prompt, mem_sample, g_mix, w_in, qk_gain, sink, rpb,
           o_gain, w_out, g_mem, w_mem_kv, g_ff, w_ff1, w_ff2):
    ones = jnp.asarray(np.kron(np.eye(LANE_TILE // HEAD_DIM), np.ones((HEAD_DIM, HEAD_DIM))), BF16)
    params = {"ones": ones, "wbias": _window_bias_table()}
    for l in range(DEPTH):
        params[l] = {
            "g_mix": g_mix[l].astype(F32)[None, :],
            "w_in": _pack_in_proj(w_in[l]),
            "qk_gain": _pack_qk_gain(qk_gain[l]),
            "nat": _neighbourhood_bias_table(rpb[l]),
            "sink": sink[l].astype(F32) * LOG2E,
            "o_gain": o_gain[l].astype(F32)[None, :],
            "w_out": w_out[l].astype(BF16),
            "g_ff": g_ff[l].astype(F32)[None, :],
            "w_ff1": w_ff1[l].astype(BF16),
            "w_ff2": w_ff2[l].astype(BF16),
        }
    km_gain = jnp.tile(qk_gain[:, 5].astype(F32), (1, M_HEADS))[:, None, :]
    g_mem3 = g_mem.astype(F32)[:, None, :]
    w_mem = w_mem_kv.astype(BF16)
    kvm_prompt = _mem_kv(mem_prompt, g_mem3, w_mem, km_gain, ones)
    kvm_sample = _mem_kv(mem_sample, g_mem3, w_mem, km_gain, ones)
    return (_trunk(x_prompt, kvm_prompt, params), _trunk(x_sample, kvm_sample, params))
```

```python
import functools

import numpy as np
import jax
import jax.numpy as jnp
from jax import lax
from jax.experimental import pallas as pl
from jax.experimental.pallas import tpu as pltpu

D_MODEL = 1024
DEPTH = 2
HEAD_DIM = 64
A_Q_HEADS = 8
A_KV_HEADS = 2
WINDOW = 128
A_BLOCK = 128
B_HEADS = 4
GRID_W = 64
NB_ROWS = 8
NB_COLS = 16
M_HEADS = 4
N_MEM = 256
D_FF = 4 * D_MODEL
EPS = 1e-6

A_Q = A_Q_HEADS * HEAD_DIM
A_KV = A_KV_HEADS * HEAD_DIM
B_W = B_HEADS * HEAD_DIM
M_W = M_HEADS * HEAD_DIM

Q_W = A_Q + B_W + M_W
NORMED_W = Q_W + 2 * 256
PROJ_W = NORMED_W + 2 * 256
LANE_TILE = 256
KV_W = 512

NAT_HALO = NB_ROWS * GRID_W
MASKED = 1e30
LOG2E = float(np.log2(np.e))

PROJ_TILE = 1024
ATTN_TILE = 1024
FFN_TILE = 1024
FF_CHUNK = 1024
VMEM_LIMIT_BYTES = 58 * 1024 * 1024

F32 = jnp.float32
BF16 = jnp.bfloat16


def _dot(a, b):
    return jnp.dot(a, b, preferred_element_type=F32)


def _dot_nt(a, b):
    return lax.dot_general(a, b, (((1,), (1,)), ((), ())), preferred_element_type=F32)


def _rms_scale(x):
    return x * lax.rsqrt(jnp.mean(x * x, axis=-1, keepdims=True) + EPS)


def _head_norm(p, ones, gain):
    ss = _dot((p * p).astype(BF16), ones)
    return p * lax.rsqrt(ss * (1.0 / HEAD_DIM) + EPS) * gain


def _lane_mask(width, lo, hi, dtype):
    lane = lax.broadcasted_iota(jnp.int32, (1, width), 1)
    return jnp.where((lane >= lo) & (lane < hi), 1.0, 0.0).astype(dtype)


def _by_head(parts):
    lane = lax.broadcasted_iota(jnp.int32, (1, 4 * HEAD_DIM), 1)
    return jnp.where(lane < 2 * HEAD_DIM,
                     jnp.where(lane < HEAD_DIM, parts[0], parts[1]),
                     jnp.where(lane < 3 * HEAD_DIM, parts[2], parts[3]))


def _mem_kv_kernel(mem_ref, g_ref, w_ref, gain_ref, ones_ref, out_ref):
    h = (_rms_scale(mem_ref[...]) * g_ref[...]).astype(BF16)
    mkv = _dot(h, w_ref[...])
    out_ref[:, 0:M_W] = _head_norm(mkv[:, 0:M_W], ones_ref[...], gain_ref[...]).astype(BF16)
    out_ref[:, M_W:2 * M_W] = mkv[:, M_W:2 * M_W].astype(BF16)


def _mem_kv(mem, g_mem, w_mem_kv, km_gain, ones):
    batch = mem.shape[0]
    return pl.pallas_call(
        _mem_kv_kernel,
        grid=(DEPTH, batch),
        in_specs=[
            pl.BlockSpec((None, N_MEM, D_MODEL), lambda l, b: (b, 0, 0)),
            pl.BlockSpec((None, 1, D_MODEL), lambda l, b: (l, 0, 0)),
            pl.BlockSpec((None, D_MODEL, 2 * M_W), lambda l, b: (l, 0, 0)),
            pl.BlockSpec((None, 1, M_W), lambda l, b: (l, 0, 0)),
            pl.BlockSpec((LANE_TILE, LANE_TILE), lambda l, b: (0, 0)),
        ],
        out_specs=pl.BlockSpec((None, None, N_MEM, 2 * M_W), lambda l, b: (l, b, 0, 0)),
        out_shape=jax.ShapeDtypeStruct((DEPTH, batch, N_MEM, 2 * M_W), BF16),
        compiler_params=pltpu.CompilerParams(dimension_semantics=("arbitrary", "arbitrary")),
        name="mem_kv",
    )(mem, g_mem, w_mem_kv, km_gain, ones)


def _in_proj_kernel(x_ref, g_ref, w_ref, gain_ref, ones_ref, q_ref, kva_ref, kvb_ref):
    h = (_rms_scale(x_ref[...]) * g_ref[...]).astype(BF16)
    ones = ones_ref[...]
    outs = (q_ref, q_ref, q_ref, q_ref, kva_ref, kvb_ref, kva_ref, kvb_ref)
    offs = (0, 256, 512, 768, 0, 0, 256, 256)
    for pair in range(PROJ_W // (2 * LANE_TILE)):
        proj = _dot(h, w_ref[:, 2 * pair * LANE_TILE:2 * (pair + 1) * LANE_TILE])
        for half in range(2):
            c = 2 * pair + half
            chunk = proj[:, half * LANE_TILE:(half + 1) * LANE_TILE]
            if c * LANE_TILE < NORMED_W:
                chunk = _head_norm(chunk, ones, gain_ref[:, c * LANE_TILE:(c + 1) * LANE_TILE])
            outs[c][:, offs[c]:offs[c] + LANE_TILE] = chunk.astype(BF16)


def _in_proj(x2d, g_mix_l, w_in_l, gain_l, ones):
    tokens = x2d.shape[0]
    tm = PROJ_TILE
    assert tokens % tm == 0
    const = lambda i: (0, 0)
    return pl.pallas_call(
        _in_proj_kernel,
        grid=(tokens // tm,),
        in_specs=[
            pl.BlockSpec((tm, D_MODEL), lambda i: (i, 0)),
            pl.BlockSpec((1, D_MODEL), const),
            pl.BlockSpec((D_MODEL, PROJ_W), const),
            pl.BlockSpec((1, NORMED_W), const),
            pl.BlockSpec((LANE_TILE, LANE_TILE), const),
        ],
        out_specs=[
            pl.BlockSpec((tm, Q_W), lambda i: (i, 0)),
            pl.BlockSpec((tm, KV_W), lambda i: (i, 0)),
            pl.BlockSpec((tm, KV_W), lambda i: (i, 0)),
        ],
        out_shape=[
            jax.ShapeDtypeStruct((tokens, Q_W), BF16),
            jax.ShapeDtypeStruct((tokens, KV_W), BF16),
            jax.ShapeDtypeStruct((tokens, KV_W), BF16),
        ],
        compiler_params=pltpu.CompilerParams(
            dimension_semantics=("arbitrary",), vmem_limit_bytes=VMEM_LIMIT_BYTES),
        name="in_proj",
    )(x2d, g_mix_l, w_in_l, gain_l, ones)


def _attn_kernel(seq, tq,
                 x_ref, q_ref, kap_ref, kac_ref, kan_ref, kbp_ref, kbc_ref, kbn_ref, kvm_ref,
                 wbias_ref, nat_ref, sink_ref, og_ref, wout_ref, out_ref,
                 kw_buf, vw_buf, kb_buf, o_buf):
    i = pl.program_id(1)
    n_blocks = seq // A_BLOCK
    n_rows = seq // GRID_W
    blocks_per_tile = tq // A_BLOCK
    rows_per_tile = tq // GRID_W

    lo_b = _lane_mask(128, 0, HEAD_DIM, BF16)
    hi_b = _lane_mask(128, HEAD_DIM, 128, BF16)
    for src, r0, n in ((kap_ref, 0, A_BLOCK), (kac_ref, A_BLOCK, tq), (kan_ref, A_BLOCK + tq, A_BLOCK)):
        rows = slice(r0, r0 + n)
        for g in range(A_KV_HEADS):
            k2 = src[:, 128 * g:128 * (g + 1)]
            v2 = src[:, 256 + 128 * g:256 + 128 * (g + 1)]
            kw_buf[rows, 256 * g:256 * g + 128] = k2 * lo_b
            kw_buf[rows, 256 * g + 128:256 * (g + 1)] = k2 * hi_b
            vw_buf[rows, 512 * g:512 * g + 128] = v2 * lo_b
            vw_buf[rows, 512 * g + 128:512 * g + 256] = jnp.broadcast_to(_lane_mask(128, 0, 1, BF16), (n, 128))
            vw_buf[rows, 512 * g + 256:512 * g + 384] = v2 * hi_b
            vw_buf[rows, 512 * g + 384:512 * (g + 1)] = jnp.broadcast_to(_lane_mask(128, 1, 2, BF16), (n, 128))
    kb_buf[0:NAT_HALO] = kbp_ref[...]
    kb_buf[NAT_HALO:NAT_HALO + tq] = kbc_ref[...]
    kb_buf[NAT_HALO + tq:2 * NAT_HALO + tq] = kbn_ref[...]

    top_rows = lax.broadcasted_iota(jnp.int32, (2 * A_BLOCK, 1), 0) < A_BLOCK
    low_lanes = lax.broadcasted_iota(jnp.int32, (1, 128), 1) < HEAD_DIM

    def window_block(j, carry):
        blk = i * blocks_per_tile + j
        variant = jnp.where(blk == 0, 1, jnp.where(blk == n_blocks - 1, 2, 0))
        q0 = pl.multiple_of(j * A_BLOCK, A_BLOCK)
        band = pl.ds(q0, 3 * A_BLOCK)
        for g in range(A_KV_HEADS):
            qg = q_ref[pl.ds(q0, A_BLOCK), 256 * g:256 * (g + 1)]
            lhs = jnp.concatenate([qg[:, 0:128], qg[:, 128:256]], axis=0)
            keys = jnp.concatenate(
                [kw_buf[band, 256 * g:256 * g + 128], kw_buf[band, 256 * g + 128:256 * (g + 1)]], axis=0)
            s = _dot_nt(lhs, keys) + wbias_ref[variant, g]
            sink_lo = jnp.where(top_rows, sink_ref[4 * g], sink_ref[4 * g + 2])
            sink_hi = jnp.where(top_rows, sink_ref[4 * g + 1], sink_ref[4 * g + 3])
            s_lo = s[:, 0:3 * A_BLOCK]
            s_hi = s[:, 3 * A_BLOCK:6 * A_BLOCK]
            m_lo = jnp.max(s_lo, axis=-1, keepdims=True)
            m_hi = jnp.max(s_hi, axis=-1, keepdims=True)
            p = jnp.concatenate([jnp.exp2(s_lo - m_lo), jnp.exp2(s_hi - m_hi)], axis=1).astype(BF16)
            vals = jnp.concatenate(
                [vw_buf[band, 512 * g:512 * g + 256], vw_buf[band, 512 * g + 256:512 * (g + 1)]], axis=0)
            o = _dot(p, vals)
            l_lo = o[:, 128:129] + jnp.exp2(sink_lo - m_lo)
            l_hi = o[:, 129:130] + jnp.exp2(sink_hi - m_hi)
            res = o[:, 0:128] / jnp.where(low_lanes, l_lo, l_hi)
            o_buf[pl.ds(q0, A_BLOCK), 256 * g:256 * g + 128] = res[0:A_BLOCK]
            o_buf[pl.ds(q0, A_BLOCK), 256 * g + 128:256 * (g + 1)] = res[A_BLOCK:2 * A_BLOCK]
        return carry

    lax.fori_loop(0, blocks_per_tile, window_block, 0, unroll=blocks_per_tile)

    head_b = [_lane_mask(B_W, HEAD_DIM * h, HEAD_DIM * (h + 1), BF16) for h in range(B_HEADS)]

    def nat_scores(rr):
        r = i * rows_per_tile + rr
        start = jnp.clip(r - NB_ROWS // 2, 0, n_rows - NB_ROWS)
        d0 = (NB_ROWS - 1) - (r - start)
        k0 = pl.multiple_of((start - i * rows_per_tile + NB_ROWS) * GRID_W, GRID_W)
        kwin = kb_buf[pl.ds(k0, NAT_HALO), 0:B_W]
        qrow = q_ref[rr * GRID_W:(rr + 1) * GRID_W, A_Q:A_Q + B_W]
        lhs = jnp.concatenate([qrow * head_b[h] for h in range(B_HEADS)], axis=0)
        bias = jnp.concatenate(
            [jnp.concatenate([nat_ref[h, d0 + 2 * m] for m in range(NB_ROWS // 2)], axis=1)
             for h in range(B_HEADS)], axis=0)
        return _dot_nt(lhs, kwin) + bias, k0

    def nat_finish(rr, s, k0):
        vwin = kb_buf[pl.ds(k0, NAT_HALO), B_W:2 * B_W]
        p = jnp.exp2(s - jnp.max(s, axis=-1, keepdims=True))
        l = jnp.sum(p, axis=-1, keepdims=True)
        pv = _dot(p.astype(BF16), vwin)
        rows = [slice(GRID_W * h, GRID_W * (h + 1)) for h in range(B_HEADS)]
        o_buf[rr * GRID_W:(rr + 1) * GRID_W, A_Q:A_Q + B_W] = (
            _by_head([pv[r] for r in rows]) / _by_head([l[r] for r in rows]))

    pending = nat_scores(0)
    for rr in range(1, rows_per_tile):
        ahead = nat_scores(rr)
        nat_finish(rr - 1, *pending)
        pending = ahead
    nat_finish(rows_per_tile - 1, *pending)

    km = kvm_ref[:, 0:M_W]
    vm = kvm_ref[:, M_W:2 * M_W]
    qm = q_ref[:, A_Q + B_W:Q_W]
    s = _dot_nt(qm, jnp.concatenate([km * head_b[h] for h in range(M_HEADS)], axis=0))
    probs, sums = [], []
    for h in range(M_HEADS):
        s_h = s[:, N_MEM * h:N_MEM * (h + 1)]
        p = jnp.exp2(s_h - jnp.max(s_h, axis=-1, keepdims=True))
        sums.append(jnp.sum(p, axis=-1, keepdims=True))
        probs.append(p.astype(BF16))
    om = _dot(jnp.concatenate(probs, axis=1),
              jnp.concatenate([vm * head_b[h] for h in range(M_HEADS)], axis=0))
    o_buf[:, A_Q + B_W:Q_W] = om / _by_head(sums)

    o = jnp.concatenate(
        [_rms_scale(o_buf[:, 0:A_Q]), _rms_scale(o_buf[:, A_Q:A_Q + B_W]), _rms_scale(o_buf[:, A_Q + B_W:Q_W])],
        axis=-1) * og_ref[...]
    out_ref[...] = x_ref[...] + _dot(o.astype(BF16), wout_ref[...])


def _attention(x, q, kva, kvb, kvm_l, wbias_tab, nat_tab, sink_l, og_l, wout_l):
    batch, seq, _ = x.shape
    tq = ATTN_TILE
    assert seq % tq == 0 and tq % NAT_HALO == 0 and seq >= 2 * A_BLOCK and seq >= NB_ROWS * GRID_W
    nt = seq // tq
    a_per_tile = tq // A_BLOCK
    n_ablocks = seq // A_BLOCK
    b_per_tile = tq // NAT_HALO
    n_bblocks = seq // NAT_HALO
    const2 = lambda b, i: (0, 0)
    const4 = lambda b, i: (0, 0, 0, 0)
    once = pl.Buffered(1)
    return pl.pallas_call(
        functools.partial(_attn_kernel, seq, tq),
        grid=(batch, nt),
        in_specs=[
            pl.BlockSpec((None, tq, D_MODEL), lambda b, i: (b, i, 0)),
            pl.BlockSpec((None, tq, Q_W), lambda b, i: (b, i, 0)),
            pl.BlockSpec((None, A_BLOCK, KV_W), lambda b, i: (b, jnp.maximum(i * a_per_tile - 1, 0), 0)),
            pl.BlockSpec((None, tq, KV_W), lambda b, i: (b, i, 0)),
            pl.BlockSpec((None, A_BLOCK, KV_W), lambda b, i: (b, jnp.minimum((i + 1) * a_per_tile, n_ablocks - 1), 0)),
            pl.BlockSpec((None, NAT_HALO, KV_W), lambda b, i: (b, jnp.maximum(i * b_per_tile - 1, 0), 0)),
            pl.BlockSpec((None, tq, KV_W), lambda b, i: (b, i, 0)),
            pl.BlockSpec((None, NAT_HALO, KV_W), lambda b, i: (b, jnp.minimum((i + 1) * b_per_tile, n_bblocks - 1), 0)),
            pl.BlockSpec((None, N_MEM, 2 * M_W), lambda b, i: (b, 0, 0)),
            pl.BlockSpec((3, A_KV_HEADS, 2 * A_BLOCK, 6 * A_BLOCK), const4, pipeline_mode=once),
            pl.BlockSpec((B_HEADS, 2 * NB_ROWS - 2, GRID_W, 2 * GRID_W), const4, pipeline_mode=once),
            pl.BlockSpec(memory_space=pltpu.SMEM),
            pl.BlockSpec((1, Q_W), const2),
            pl.BlockSpec((Q_W, D_MODEL), const2, pipeline_mode=once),
        ],
        out_specs=pl.BlockSpec((None, tq, D_MODEL), lambda b, i: (b, i, 0)),
        out_shape=jax.ShapeDtypeStruct((batch, seq, D_MODEL), F32),
        scratch_shapes=[
            pltpu.VMEM((tq + 2 * A_BLOCK, 2 * 256), BF16),
            pltpu.VMEM((tq + 2 * A_BLOCK, 2 * 512), BF16),
            pltpu.VMEM((tq + 2 * NAT_HALO, KV_W), BF16),
            pltpu.VMEM((tq, Q_W), F32),
        ],
        compiler_params=pltpu.CompilerParams(
            dimension_semantics=("arbitrary", "arbitrary"), vmem_limit_bytes=VMEM_LIMIT_BYTES),
        name="attention",
    )(x, q, kva, kva, kva, kvb, kvb, kvb, kvm_l, wbias_tab, nat_tab, sink_l, og_l, wout_l)


def _ffn_kernel(x_ref, g_ref, w1_ref, w2_ref, out_ref):
    x = x_ref[...]
    h = (_rms_scale(x) * g_ref[...]).astype(BF16)
    acc = x
    for c in range(D_FF // FF_CHUNK):
        f = _dot(h, w1_ref[:, c * FF_CHUNK:(c + 1) * FF_CHUNK])
        a = jnp.square(jnp.maximum(f, 0.0)).astype(BF16)
        acc = acc + _dot(a, w2_ref[c * FF_CHUNK:(c + 1) * FF_CHUNK, :])
    out_ref[...] = acc


def _ffn(x2d, g_ff_l, w1_l, w2_l):
    tokens = x2d.shape[0]
    tm = FFN_TILE
    assert tokens % tm == 0
    const = lambda i: (0, 0)
    return pl.pallas_call(
        _ffn_kernel,
        grid=(tokens // tm,),
        in_specs=[
            pl.BlockSpec((tm, D_MODEL), lambda i: (i, 0)),
            pl.BlockSpec((1, D_MODEL), const),
            pl.BlockSpec((D_MODEL, D_FF), const, pipeline_mode=pl.Buffered(1)),
            pl.BlockSpec((D_FF, D_MODEL), const, pipeline_mode=pl.Buffered(1)),
        ],
        out_specs=pl.BlockSpec((tm, D_MODEL), lambda i: (i, 0)),
        out_shape=jax.ShapeDtypeStruct((tokens, D_MODEL), F32),
        compiler_params=pltpu.CompilerParams(
            dimension_semantics=("arbitrary",), vmem_limit_bytes=VMEM_LIMIT_BYTES),
        name="ffn",
    )(x2d, g_ff_l, w1_l, w2_l)


def _window_bias_table():
    qi = np.arange(A_BLOCK)[:, None]
    si = np.arange(3 * A_BLOCK)[None, :]
    dist = np.abs(si - A_BLOCK - qi).astype(np.float32)
    band = dist <= WINDOW
    keep = [band, band & (si >= A_BLOCK), band & (si < 2 * A_BLOCK)]
    slopes = (2.0 ** (-8.0 * np.arange(1, A_Q_HEADS + 1) / A_Q_HEADS)).astype(np.float32)
    tab = np.empty((3, A_KV_HEADS, 2 * A_BLOCK, 6 * A_BLOCK), np.float32)
    for v in range(3):
        for g in range(A_KV_HEADS):
            for hh in range(4):
                rows = slice(A_BLOCK * (hh // 2), A_BLOCK * (hh // 2 + 1))
                cols = slice(3 * A_BLOCK * (hh % 2), 3 * A_BLOCK * (hh % 2 + 1))
                tab[v, g, rows, cols] = np.where(keep[v], -(slopes[4 * g + hh] * dist) * LOG2E, -MASKED)
    return jnp.asarray(tab)


def _neighbourhood_bias_table(rpb_l):
    c = np.arange(GRID_W)
    dc = np.clip(c[None, :] - c[:, None] + (NB_COLS - 1), 0, 2 * NB_COLS - 2)
    onehot = jnp.asarray((dc[None] == np.arange(2 * NB_COLS - 1)[:, None, None]).astype(np.float32))
    t = jnp.einsum("hdc,cqk->hdqk", rpb_l.astype(F32), onehot, precision=lax.Precision.HIGHEST) * LOG2E
    cstart = np.clip(c - NB_COLS // 2, 0, GRID_W - NB_COLS)
    cvalid = (c[None, :] >= cstart[:, None]) & (c[None, :] < cstart[:, None] + NB_COLS)
    t = jnp.where(cvalid[None, None], t, -MASKED)
    return jnp.concatenate([t[:, :-1], t[:, 1:]], axis=-1)


def _pack_in_proj(w_in_l):
    qa, ka, va, qb, kb, vb, qm = jnp.split(
        w_in_l, [A_Q, A_Q + A_KV, A_Q + 2 * A_KV, A_Q + 2 * A_KV + B_W,
                 A_Q + 2 * A_KV + 2 * B_W, A_Q + 2 * A_KV + 3 * B_W], axis=1)
    twice = lambda t: jnp.concatenate(
        [t[:, :HEAD_DIM], t[:, :HEAD_DIM], t[:, HEAD_DIM:], t[:, HEAD_DIM:]], axis=1)
    return jnp.concatenate([qa, qb, qm, twice(ka), kb, twice(va), vb], axis=1).astype(BF16)


def _pack_qk_gain(qk_gain_l):
    scale = HEAD_DIM ** -0.5 * LOG2E
    g = qk_gain_l.astype(F32)
    return jnp.concatenate([
        jnp.tile(g[0], A_Q_HEADS) * scale, jnp.tile(g[2], B_HEADS) * scale, jnp.tile(g[4], M_HEADS) * scale,
        jnp.tile(g[1], 2 * A_KV_HEADS), jnp.tile(g[3], B_HEADS)])[None, :]


def _trunk(x, kvm, params):
    batch, seq, _ = x.shape
    for l in range(DEPTH):
        p = params[l]
        q, kva, kvb = _in_proj(x.reshape(batch * seq, D_MODEL), p["g_mix"], p["w_in"], p["qk_gain"], params["ones"])
        x = _attention(x, q.reshape(batch, seq, Q_W), kva.reshape(batch, seq, KV_W), kvb.reshape(batch, seq, KV_W),
                       kvm[l], params["wbias"], p["nat"], p["sink"], p["o_gain"], p["w_out"])
        x = _ffn(x.reshape(batch * seq, D_MODEL), p["g_ff"], p["w_ff1"], p["w_ff2"]).reshape(batch, seq, D_MODEL)
    return x


def kernel(x_prompt, x_sample, mem_prompt, mem_sample, g_mix, w_in, qk_gain, sink, rpb,
           o_gain, w_out, g_mem, w_mem_kv, g_ff, w_ff1, w_ff2):
    ones = jnp.asarray(np.kron(np.eye(LANE_TILE // HEAD_DIM), np.ones((HEAD_DIM, HEAD_DIM))), BF16)
    params = {"ones": ones, "wbias": _window_bias_table()}
    for l in range(DEPTH):
        params[l] = {
            "g_mix": g_mix[l].astype(F32)[None, :],
            "w_in": _pack_in_proj(w_in[l]),
            "qk_gain": _pack_qk_gain(qk_gain[l]),
            "nat": _neighbourhood_bias_table(rpb[l]),
            "sink": sink[l].astype(F32) * LOG2E,
            "o_gain": o_gain[l].astype(F32)[None, :],
            "w_out": w_out[l].astype(BF16),
            "g_ff": g_ff[l].astype(F32)[None, :],
            "w_ff1": w_ff1[l].astype(BF16),
            "w_ff2": w_ff2[l].astype(BF16),
        }
    km_gain = jnp.tile(qk_gain[:, 5].astype(F32), (1, M_HEADS))[:, None, :]
    g_mem3 = g_mem.astype(F32)[:, None, :]
    w_mem = w_mem_kv.astype(BF16)
    kvm_prompt = _mem_kv(mem_prompt, g_mem3, w_mem, km_gain, ones)
    kvm_sample = _mem_kv(mem_sample, g_mem3, w_mem, km_gain, ones)
    return (_trunk(x_prompt, kvm_prompt, params), _trunk(x_sample, kvm_sample, params))
```

```python
import functools

import numpy as np
import jax
import jax.numpy as jnp
from jax import lax
from jax.experimental import pallas as pl
from jax.experimental.pallas import tpu as pltpu

D_MODEL = 1024
DEPTH = 2
HEAD_DIM = 64
A_Q_HEADS = 8
A_KV_HEADS = 2
WINDOW = 128
A_BLOCK = 128
B_HEADS = 4
GRID_W = 64
NB_ROWS = 8
NB_COLS = 16
M_HEADS = 4
N_MEM = 256
D_FF = 4 * D_MODEL
EPS = 1e-6

A_Q = A_Q_HEADS * HEAD_DIM
A_KV = A_KV_HEADS * HEAD_DIM
B_W = B_HEADS * HEAD_DIM
M_W = M_HEADS * HEAD_DIM

Q_W = A_Q + B_W + M_W
NORMED_W = Q_W + 2 * 256
PROJ_W = NORMED_W + 2 * 256
LANE_TILE = 256
KV_W = 512
KVA_OFF = Q_W
KVB_OFF = Q_W + KV_W

NAT_HALO = NB_ROWS * GRID_W
MASKED = 1e30
LOG2E = float(np.log2(np.e))

PROJ_TILE = 1024
PROJ_SUBTILES = 2
ATTN_TILE = 1024
FFN_TILE = 1024
FF_CHUNK = 1024
VMEM_LIMIT_BYTES = 58 * 1024 * 1024

F32 = jnp.float32
BF16 = jnp.bfloat16


def _dot(a, b):
    return jnp.dot(a, b, preferred_element_type=F32)


def _dot_nt(a, b):
    return lax.dot_general(a, b, (((1,), (1,)), ((), ())), preferred_element_type=F32)


def _rms_scale(x):
    return x * lax.rsqrt(jnp.mean(x * x, axis=-1, keepdims=True) + EPS)


def _head_norm(p, ones, gain):
    ms = _dot((p * p).astype(BF16), ones)
    return p * lax.rsqrt(ms + EPS) * gain


def _lane_mask(width, lo, hi, dtype):
    lane = lax.broadcasted_iota(jnp.int32, (1, width), 1)
    return jnp.where((lane >= lo) & (lane < hi), 1.0, 0.0).astype(dtype)


def _by_head(parts):
    lane = lax.broadcasted_iota(jnp.int32, (1, 4 * HEAD_DIM), 1)
    return jnp.where(lane < 2 * HEAD_DIM,
                     jnp.where(lane < HEAD_DIM, parts[0], parts[1]),
                     jnp.where(lane < 3 * HEAD_DIM, parts[2], parts[3]))


def _mem_kv_kernel(mem_ref, g_ref, w_ref, gain_ref, ones_ref, out_ref):
    h = (_rms_scale(mem_ref[...]) * g_ref[...]).astype(BF16)
    mkv = _dot(h, w_ref[...])
    out_ref[:, 0:M_W] = _head_norm(mkv[:, 0:M_W], ones_ref[...], gain_ref[...]).astype(BF16)
    out_ref[:, M_W:2 * M_W] = mkv[:, M_W:2 * M_W].astype(BF16)


def _mem_kv(mem, g_mem, w_mem_kv, km_gain, ones):
    batch = mem.shape[0]
    return pl.pallas_call(
        _mem_kv_kernel,
        grid=(DEPTH, batch),
        in_specs=[
            pl.BlockSpec((None, N_MEM, D_MODEL), lambda l, b: (b, 0, 0)),
            pl.BlockSpec((None, 1, D_MODEL), lambda l, b: (l, 0, 0)),
            pl.BlockSpec((None, D_MODEL, 2 * M_W), lambda l, b: (l, 0, 0)),
            pl.BlockSpec((None, 1, M_W), lambda l, b: (l, 0, 0)),
            pl.BlockSpec((LANE_TILE, LANE_TILE), lambda l, b: (0, 0)),
        ],
        out_specs=pl.BlockSpec((None, None, N_MEM, 2 * M_W), lambda l, b: (l, b, 0, 0)),
        out_shape=jax.ShapeDtypeStruct((DEPTH, batch, N_MEM, 2 * M_W), BF16),
        compiler_params=pltpu.CompilerParams(dimension_semantics=("arbitrary", "arbitrary")),
        name="mem_kv",
    )(mem, g_mem, w_mem_kv, km_gain, ones)


def _in_proj_kernel(x_ref, g_ref, w_ref, gain_ref, ones_ref, qkv_ref):
    ones = ones_ref[...]
    offs = (0, 256, 512, 768, KVA_OFF, KVB_OFF, KVA_OFF + 256, KVB_OFF + 256)
    sub = x_ref.shape[0] // PROJ_SUBTILES
    for t in range(PROJ_SUBTILES):
        rows = slice(t * sub, (t + 1) * sub)
        h = (_rms_scale(x_ref[rows, :]) * g_ref[...]).astype(BF16)
        for pair in range(PROJ_W // (2 * LANE_TILE)):
            proj = _dot(h, w_ref[:, 2 * pair * LANE_TILE:2 * (pair + 1) * LANE_TILE])
            for half in range(2):
                c = 2 * pair + half
                chunk = proj[:, half * LANE_TILE:(half + 1) * LANE_TILE]
                if c * LANE_TILE < NORMED_W:
                    chunk = _head_norm(chunk, ones, gain_ref[:, c * LANE_TILE:(c + 1) * LANE_TILE])
                qkv_ref[rows, offs[c]:offs[c] + LANE_TILE] = chunk.astype(BF16)


def _in_proj(x2d, g_mix_l, w_in_l, gain_l, ones):
    tokens = x2d.shape[0]
    tm = PROJ_TILE
    assert tokens % tm == 0
    const = lambda i: (0, 0)
    return pl.pallas_call(
        _in_proj_kernel,
        grid=(tokens // tm,),
        in_specs=[
            pl.BlockSpec((tm, D_MODEL), lambda i: (i, 0)),
            pl.BlockSpec((1, D_MODEL), const),
            pl.BlockSpec((D_MODEL, PROJ_W), const),
            pl.BlockSpec((1, NORMED_W), const),
            pl.BlockSpec((LANE_TILE, LANE_TILE), const),
        ],
        out_specs=pl.BlockSpec((tm, PROJ_W), lambda i: (i, 0)),
        out_shape=jax.ShapeDtypeStruct((tokens, PROJ_W), BF16),
        compiler_params=pltpu.CompilerParams(
            dimension_semantics=("arbitrary",), vmem_limit_bytes=VMEM_LIMIT_BYTES),
        name="in_proj",
    )(x2d, g_mix_l, w_in_l, gain_l, ones)


def _attn_kernel(seq, tq,
                 x_ref, q_ref, prev_ref, next_ref, kvm_ref,
                 wbias_ref, nat_ref, sink_ref, og_ref, wout_ref, out_ref,
                 kw_buf, vw_buf, kb_buf, o_buf):
    i = pl.program_id(1)
    n_blocks = seq // A_BLOCK
    n_rows = seq // GRID_W
    blocks_per_tile = tq // A_BLOCK
    rows_per_tile = tq // GRID_W

    lo_b = _lane_mask(128, 0, HEAD_DIM, BF16)
    hi_b = _lane_mask(128, HEAD_DIM, 128, BF16)
    pieces = ((prev_ref, slice(NAT_HALO - A_BLOCK, NAT_HALO), 0, 0, A_BLOCK),
              (q_ref, slice(0, tq), KVA_OFF, A_BLOCK, tq),
              (next_ref, slice(0, A_BLOCK), 0, A_BLOCK + tq, A_BLOCK))
    for src, src_rows, lane0, r0, n in pieces:
        rows = slice(r0, r0 + n)
        for g in range(A_KV_HEADS):
            k2 = src[src_rows, lane0 + 128 * g:lane0 + 128 * (g + 1)]
            v2 = src[src_rows, lane0 + 256 + 128 * g:lane0 + 256 + 128 * (g + 1)]
            kw_buf[rows, 256 * g:256 * g + 128] = k2 * lo_b
            kw_buf[rows, 256 * g + 128:256 * (g + 1)] = k2 * hi_b
            vw_buf[rows, 512 * g:512 * g + 128] = v2 * lo_b
            vw_buf[rows, 512 * g + 128:512 * g + 256] = jnp.broadcast_to(_lane_mask(128, 0, 1, BF16), (n, 128))
            vw_buf[rows, 512 * g + 256:512 * g + 384] = v2 * hi_b
            vw_buf[rows, 512 * g + 384:512 * (g + 1)] = jnp.broadcast_to(_lane_mask(128, 1, 2, BF16), (n, 128))
    kb_buf[0:NAT_HALO] = prev_ref[:, KV_W:2 * KV_W]
    kb_buf[NAT_HALO:NAT_HALO + tq] = q_ref[:, KVB_OFF:KVB_OFF + KV_W]
    kb_buf[NAT_HALO + tq:2 * NAT_HALO + tq] = next_ref[:, KV_W:2 * KV_W]

    top_rows = lax.broadcasted_iota(jnp.int32, (2 * A_BLOCK, 1), 0) < A_BLOCK
    low_lanes = lax.broadcasted_iota(jnp.int32, (1, 128), 1) < HEAD_DIM

    def window_block(j, carry):
        blk = i * blocks_per_tile + j
        variant = jnp.where(blk == 0, 1, jnp.where(blk == n_blocks - 1, 2, 0))
        q0 = pl.multiple_of(j * A_BLOCK, A_BLOCK)
        band = pl.ds(q0, 3 * A_BLOCK)
        for g in range(A_KV_HEADS):
            qg = q_ref[pl.ds(q0, A_BLOCK), 256 * g:256 * (g + 1)]
            lhs = jnp.concatenate([qg[:, 0:128], qg[:, 128:256]], axis=0)
            keys = jnp.concatenate(
                [kw_buf[band, 256 * g:256 * g + 128], kw_buf[band, 256 * g + 128:256 * (g + 1)]], axis=0)
            s = _dot_nt(lhs, keys) + wbias_ref[variant, g]
            sink_lo = jnp.where(top_rows, sink_ref[4 * g], sink_ref[4 * g + 2])
            sink_hi = jnp.where(top_rows, sink_ref[4 * g + 1], sink_ref[4 * g + 3])
            s_lo = s[:, 0:3 * A_BLOCK]
            s_hi = s[:, 3 * A_BLOCK:6 * A_BLOCK]
            m_lo = jnp.max(s_lo, axis=-1, keepdims=True)
            m_hi = jnp.max(s_hi, axis=-1, keepdims=True)
            p = jnp.concatenate([jnp.exp2(s_lo - m_lo), jnp.exp2(s_hi - m_hi)], axis=1).astype(BF16)
            vals = jnp.concatenate(
                [vw_buf[band, 512 * g:512 * g + 256], vw_buf[band, 512 * g + 256:512 * (g + 1)]], axis=0)
            o = _dot(p, vals)
            l_lo = o[:, 128:129] + jnp.exp2(sink_lo - m_lo)
            l_hi = o[:, 129:130] + jnp.exp2(sink_hi - m_hi)
            res = o[:, 0:128] / jnp.where(low_lanes, l_lo, l_hi)
            o_buf[pl.ds(q0, A_BLOCK), 256 * g:256 * g + 128] = res[0:A_BLOCK]
            o_buf[pl.ds(q0, A_BLOCK), 256 * g + 128:256 * (g + 1)] = res[A_BLOCK:2 * A_BLOCK]
        return carry

    lax.fori_loop(0, blocks_per_tile, window_block, 0, unroll=blocks_per_tile)

    head_b = [_lane_mask(B_W, HEAD_DIM * h, HEAD_DIM * (h + 1), BF16) for h in range(B_HEADS)]

    def nat_scores(rr):
        r = i * rows_per_tile + rr
        start = jnp.clip(r - NB_ROWS // 2, 0, n_rows - NB_ROWS)
        d0 = (NB_ROWS - 1) - (r - start)
        k0 = pl.multiple_of((start - i * rows_per_tile + NB_ROWS) * GRID_W, GRID_W)
        kwin = kb_buf[pl.ds(k0, NAT_HALO), 0:B_W]
        qrow = q_ref[rr * GRID_W:(rr + 1) * GRID_W, A_Q:A_Q + B_W]
        lhs = jnp.concatenate([qrow * head_b[h] for h in range(B_HEADS)], axis=0)
        bias = jnp.concatenate(
            [jnp.concatenate([nat_ref[h, d0 + 2 * m] for m in range(NB_ROWS // 2)], axis=1)
             for h in range(B_HEADS)], axis=0)
        return _dot_nt(lhs, kwin) + bias, k0

    def nat_finish(rr, s, k0):
        vwin = kb_buf[pl.ds(k0, NAT_HALO), B_W:2 * B_W]
        p = jnp.exp2(s - jnp.max(s, axis=-1, keepdims=True))
        l = jnp.sum(p, axis=-1, keepdims=True)
        pv = _dot(p.astype(BF16), vwin)
        rows = [slice(GRID_W * h, GRID_W * (h + 1)) for h in range(B_HEADS)]
        o_buf[rr * GRID_W:(rr + 1) * GRID_W, A_Q:A_Q + B_W] = (
            _by_head([pv[r] for r in rows]) / _by_head([l[r] for r in rows]))

    pending = nat_scores(0)
    for rr in range(1, rows_per_tile):
        ahead = nat_scores(rr)
        nat_finish(rr - 1, *pending)
        pending = ahead
    nat_finish(rows_per_tile - 1, *pending)

    km = kvm_ref[:, 0:M_W]
    vm = kvm_ref[:, M_W:2 * M_W]
    qm = q_ref[:, A_Q + B_W:Q_W]
    s = _dot_nt(qm, jnp.concatenate([km * head_b[h] for h in range(M_HEADS)], axis=0))
    probs, sums = [], []
    for h in range(M_HEADS):
        s_h = s[:, N_MEM * h:N_MEM * (h + 1)]
        p = jnp.exp2(s_h - jnp.max(s_h, axis=-1, keepdims=True))
        sums.append(jnp.sum(p, axis=-1, keepdims=True))
        probs.append(p.astype(BF16))
    om = _dot(jnp.concatenate(probs, axis=1),
              jnp.concatenate([vm * head_b[h] for h in range(M_HEADS)], axis=0))
    o_buf[:, A_Q + B_W:Q_W] = om / _by_head(sums)

    o = jnp.concatenate(
        [_rms_scale(o_buf[:, 0:A_Q]), _rms_scale(o_buf[:, A_Q:A_Q + B_W]), _rms_scale(o_buf[:, A_Q + B_W:Q_W])],
        axis=-1) * og_ref[...]
    out_ref[...] = x_ref[...] + _dot(o.astype(BF16), wout_ref[...])


def _attention(x, qkv, kvm_l, wbias_tab, nat_tab, sink_l, og_l, wout_l):
    batch, seq, _ = x.shape
    tq = ATTN_TILE
    assert seq % tq == 0 and tq % NAT_HALO == 0 and seq >= 2 * A_BLOCK and seq >= NB_ROWS * GRID_W
    nt = seq // tq
    halo_per_tile = tq // NAT_HALO
    n_halo_blocks = seq // NAT_HALO
    const2 = lambda b, i: (0, 0)
    const4 = lambda b, i: (0, 0, 0, 0)
    once = pl.Buffered(1)
    return pl.pallas_call(
        functools.partial(_attn_kernel, seq, tq),
        grid=(batch, nt),
        in_specs=[
            pl.BlockSpec((None, tq, D_MODEL), lambda b, i: (b, i, 0)),
            pl.BlockSpec((None, tq, PROJ_W), lambda b, i: (b, i, 0)),
            pl.BlockSpec((None, NAT_HALO, 2 * KV_W),
                         lambda b, i: (b, jnp.maximum(i * halo_per_tile - 1, 0), 1)),
            pl.BlockSpec((None, NAT_HALO, 2 * KV_W),
                         lambda b, i: (b, jnp.minimum((i + 1) * halo_per_tile, n_halo_blocks - 1), 1)),
            pl.BlockSpec((None, N_MEM, 2 * M_W), lambda b, i: (b, 0, 0)),
            pl.BlockSpec((3, A_KV_HEADS, 2 * A_BLOCK, 6 * A_BLOCK), const4, pipeline_mode=once),
            pl.BlockSpec((B_HEADS, 2 * NB_ROWS - 2, GRID_W, 2 * GRID_W), const4, pipeline_mode=once),
            pl.BlockSpec(memory_space=pltpu.SMEM),
            pl.BlockSpec((1, Q_W), const2),
            pl.BlockSpec((Q_W, D_MODEL), const2, pipeline_mode=once),
        ],
        out_specs=pl.BlockSpec((None, tq, D_MODEL), lambda b, i: (b, i, 0)),
        out_shape=jax.ShapeDtypeStruct((batch, seq, D_MODEL), F32),
        scratch_shapes=[
            pltpu.VMEM((tq + 2 * A_BLOCK, 2 * 256), BF16),
            pltpu.VMEM((tq + 2 * A_BLOCK, 2 * 512), BF16),
            pltpu.VMEM((tq + 2 * NAT_HALO, KV_W), BF16),
            pltpu.VMEM((tq, Q_W), F32),
        ],
        compiler_params=pltpu.CompilerParams(
            dimension_semantics=("arbitrary", "arbitrary"), vmem_limit_bytes=VMEM_LIMIT_BYTES),
        name="attention",
    )(x, qkv, qkv, qkv, kvm_l, wbias_tab, nat_tab, sink_l, og_l, wout_l)


def _ffn_kernel(x_ref, g_ref, w1_ref, w2_ref, out_ref):
    x = x_ref[...]
    h = (_rms_scale(x) * g_ref[...]).astype(BF16)
    acc = x
    for c in range(D_FF // FF_CHUNK):
        f = _dot(h, w1_ref[:, c * FF_CHUNK:(c + 1) * FF_CHUNK])
        a = jnp.square(jnp.maximum(f, 0.0)).astype(BF16)
        acc = acc + _dot(a, w2_ref[c * FF_CHUNK:(c + 1) * FF_CHUNK, :])
    out_ref[...] = acc


def _ffn(x2d, g_ff_l, w1_l, w2_l):
    tokens = x2d.shape[0]
    tm = FFN_TILE
    assert tokens % tm == 0
    const = lambda i: (0, 0)
    return pl.pallas_call(
        _ffn_kernel,
        grid=(tokens // tm,),
        in_specs=[
            pl.BlockSpec((tm, D_MODEL), lambda i: (i, 0)),
            pl.BlockSpec((1, D_MODEL), const),
            pl.BlockSpec((D_MODEL, D_FF), const, pipeline_mode=pl.Buffered(1)),
            pl.BlockSpec((D_FF, D_MODEL), const, pipeline_mode=pl.Buffered(1)),
        ],
        out_specs=pl.BlockSpec((tm, D_MODEL), lambda i: (i, 0)),
        out_shape=jax.ShapeDtypeStruct((tokens, D_MODEL), F32),
        compiler_params=pltpu.CompilerParams(
            dimension_semantics=("arbitrary",), vmem_limit_bytes=VMEM_LIMIT_BYTES),
        name="ffn",
    )(x2d, g_ff_l, w1_l, w2_l)


def _window_bias_table():
    qi = np.arange(A_BLOCK)[:, None]
    si = np.arange(3 * A_BLOCK)[None, :]
    dist = np.abs(si - A_BLOCK - qi).astype(np.float32)
    band = dist <= WINDOW
    keep = [band, band & (si >= A_BLOCK), band & (si < 2 * A_BLOCK)]
    slopes = (2.0 ** (-8.0 * np.arange(1, A_Q_HEADS + 1) / A_Q_HEADS)).astype(np.float32)
    tab = np.empty((3, A_KV_HEADS, 2 * A_BLOCK, 6 * A_BLOCK), np.float32)
    for v in range(3):
        for g in range(A_KV_HEADS):
            for hh in range(4):
                rows = slice(A_BLOCK * (hh // 2), A_BLOCK * (hh // 2 + 1))
                cols = slice(3 * A_BLOCK * (hh % 2), 3 * A_BLOCK * (hh % 2 + 1))
                tab[v, g, rows, cols] = np.where(keep[v], -(slopes[4 * g + hh] * dist) * LOG2E, -MASKED)
    return jnp.asarray(tab)


def _neighbourhood_bias_table(rpb_l):
    c = np.arange(GRID_W)
    dc = np.clip(c[None, :] - c[:, None] + (NB_COLS - 1), 0, 2 * NB_COLS - 2)
    onehot = jnp.asarray((dc[None] == np.arange(2 * NB_COLS - 1)[:, None, None]).astype(np.float32))
    t = jnp.einsum("hdc,cqk->hdqk", rpb_l.astype(F32), onehot, precision=lax.Precision.HIGHEST) * LOG2E
    cstart = np.clip(c - NB_COLS // 2, 0, GRID_W - NB_COLS)
    cvalid = (c[None, :] >= cstart[:, None]) & (c[None, :] < cstart[:, None] + NB_COLS)
    t = jnp.where(cvalid[None, None], t, -MASKED)
    return jnp.concatenate([t[:, :-1], t[:, 1:]], axis=-1)


def _pack_in_proj(w_in_l):
    qa, ka, va, qb, kb, vb, qm = jnp.split(
        w_in_l, [A_Q, A_Q + A_KV, A_Q + 2 * A_KV, A_Q + 2 * A_KV + B_W,
                 A_Q + 2 * A_KV + 2 * B_W, A_Q + 2 * A_KV + 3 * B_W], axis=1)
    twice = lambda t: jnp.concatenate(
        [t[:, :HEAD_DIM], t[:, :HEAD_DIM], t[:, HEAD_DIM:], t[:, HEAD_DIM:]], axis=1)
    return jnp.concatenate([qa, qb, qm, twice(ka), kb, twice(va), vb], axis=1).astype(BF16)


def _pack_qk_gain(qk_gain_l):
    scale = HEAD_DIM ** -0.5 * LOG2E
    g = qk_gain_l.astype(F32)
    return jnp.concatenate([
        jnp.tile(g[0], A_Q_HEADS) * scale, jnp.tile(g[2], B_HEADS) * scale, jnp.tile(g[4], M_HEADS) * scale,
        jnp.tile(g[1], 2 * A_KV_HEADS), jnp.tile(g[3], B_HEADS)])[None, :]


def _trunk(x, kvm, params):
    batch, seq, _ = x.shape
    for l in range(DEPTH):
        p = params[l]
        qkv = _in_proj(x.reshape(batch * seq, D_MODEL), p["g_mix"], p["w_in"], p["qk_gain"], params["ones"])
        x = _attention(x, qkv.reshape(batch, seq, PROJ_W),
                       kvm[l], params["wbias"], p["nat"], p["sink"], p["o_gain"], p["w_out"])
        x = _ffn(x.reshape(batch * seq, D_MODEL), p["g_ff"], p["w_ff1"], p["w_ff2"]).reshape(batch, seq, D_MODEL)
    return x


def kernel(x_prompt, x_sample, mem_prompt, mem_sample, g_mix, w_in, qk_gain, sink, rpb,
           o_gain, w_out, g_mem, w_mem_kv, g_ff, w_ff1, w_ff2):
    ones = jnp.asarray(
        np.kron(np.eye(LANE_TILE // HEAD_DIM), np.full((HEAD_DIM, HEAD_DIM), 1.0 / HEAD_DIM)), BF16)
    params = {"ones": ones, "wbias": _window_bias_table()}
    for l in range(DEPTH):
        params[l] = {
            "g_mix": g_mix[l].astype(F32)[None, :],
            "w_in": _pack_in_proj(w_in[l]),
            "qk_gain": _pack_qk_gain(qk_gain[l]),
            "nat": _neighbourhood_bias_table(rpb[l]),
            "sink": sink[l].astype(F32) * LOG2E,
            "o_gain": o_gain[l].astype(F32)[None, :],
            "w_out": w_out[l].astype(BF16),
            "g_ff": g_ff[l].astype(F32)[None, :],
            "w_ff1": w_ff1[l].astype(BF16),
            "w_ff2": w_ff2[l].astype(BF16),
        }
    km_gain = jnp.tile(qk_gain[:, 5].astype(F32), (1, M_HEADS))[:, None, :]
    g_mem3 = g_mem.astype(F32)[:, None, :]
    w_mem = w_mem_kv.astype(BF16)
    kvm_prompt = _mem_kv(mem_prompt, g_mem3, w_mem, km_gain, ones)
    kvm_sample = _mem_kv(mem_sample, g_mem3, w_mem, km_gain, ones)
    return (_trunk(x_prompt, kvm_prompt, params), _trunk(x_sample, kvm_sample, params))
```

```python
import functools

import numpy as np
import jax
import jax.numpy as jnp
from jax import lax
from jax.experimental import pallas as pl
from jax.experimental.pallas import tpu as pltpu

D_MODEL = 1024
DEPTH = 2
HEAD_DIM = 64
A_Q_HEADS = 8
A_KV_HEADS = 2
WINDOW = 128
A_BLOCK = 128
B_HEADS = 4
GRID_W = 64
NB_ROWS = 8
NB_COLS = 16
M_HEADS = 4
N_MEM = 256
D_FF = 4 * D_MODEL
EPS = 1e-6

A_Q = A_Q_HEADS * HEAD_DIM
A_KV = A_KV_HEADS * HEAD_DIM
B_W = B_HEADS * HEAD_DIM
M_W = M_HEADS * HEAD_DIM

Q_W = A_Q + B_W + M_W
NORMED_W = Q_W + 2 * 256
PROJ_W = NORMED_W + 2 * 256
LANE_TILE = 256
KV_W = 512
KVA_OFF = Q_W
KVB_OFF = Q_W + KV_W

NAT_HALO = NB_ROWS * GRID_W
MASKED = 1e30
MEM_BOUND_SLOT = A_Q_HEADS
MIN_DENOMINATOR = 2.0 ** -100
LOG2E = float(np.log2(np.e))

PROJ_TILE = 1024
PROJ_SUBTILES = 2
ATTN_TILE = 1024
FFN_TILE = 1024
FF_CHUNK = 1024
VMEM_LIMIT_BYTES = 58 * 1024 * 1024

F32 = jnp.float32
BF16 = jnp.bfloat16


def _dot(a, b):
    return jnp.dot(a, b, preferred_element_type=F32)


def _dot_nt(a, b):
    return lax.dot_general(a, b, (((1,), (1,)), ((), ())), preferred_element_type=F32)


def _rms_scale(x):
    return x * lax.rsqrt(jnp.mean(x * x, axis=-1, keepdims=True) + EPS)


def _head_norm(p, ones, gain):
    ms = _dot((p * p).astype(BF16), ones)
    return p * lax.rsqrt(ms + EPS) * gain


def _lane_mask(width, lo, hi, dtype):
    lane = lax.broadcasted_iota(jnp.int32, (1, width), 1)
    return jnp.where((lane >= lo) & (lane < hi), 1.0, 0.0).astype(dtype)


def _by_head(parts):
    lane = lax.broadcasted_iota(jnp.int32, (1, 4 * HEAD_DIM), 1)
    return jnp.where(lane < 2 * HEAD_DIM,
                     jnp.where(lane < HEAD_DIM, parts[0], parts[1]),
                     jnp.where(lane < 3 * HEAD_DIM, parts[2], parts[3]))


def _mem_kv_kernel(mem_ref, g_ref, w_ref, gain_ref, ones_ref, out_ref):
    h = (_rms_scale(mem_ref[...]) * g_ref[...]).astype(BF16)
    mkv = _dot(h, w_ref[...])
    out_ref[:, 0:M_W] = _head_norm(mkv[:, 0:M_W], ones_ref[...], gain_ref[...]).astype(BF16)
    out_ref[:, M_W:2 * M_W] = mkv[:, M_W:2 * M_W].astype(BF16)


def _mem_kv(mem, g_mem, w_mem_kv, km_gain, ones):
    batch = mem.shape[0]
    return pl.pallas_call(
        _mem_kv_kernel,
        grid=(DEPTH, batch),
        in_specs=[
            pl.BlockSpec((None, N_MEM, D_MODEL), lambda l, b: (b, 0, 0)),
            pl.BlockSpec((None, 1, D_MODEL), lambda l, b: (l, 0, 0)),
            pl.BlockSpec((None, D_MODEL, 2 * M_W), lambda l, b: (l, 0, 0)),
            pl.BlockSpec((None, 1, M_W), lambda l, b: (l, 0, 0)),
            pl.BlockSpec((LANE_TILE, LANE_TILE), lambda l, b: (0, 0)),
        ],
        out_specs=pl.BlockSpec((None, None, N_MEM, 2 * M_W), lambda l, b: (l, b, 0, 0)),
        out_shape=jax.ShapeDtypeStruct((DEPTH, batch, N_MEM, 2 * M_W), BF16),
        compiler_params=pltpu.CompilerParams(dimension_semantics=("arbitrary", "arbitrary")),
        name="mem_kv",
    )(mem, g_mem, w_mem_kv, km_gain, ones)


def _in_proj_kernel(x_ref, g_ref, w_ref, gain_ref, ones_ref, qkv_ref):
    ones = ones_ref[...]
    offs = (0, 256, 512, 768, KVA_OFF, KVB_OFF, KVA_OFF + 256, KVB_OFF + 256)
    sub = x_ref.shape[0] // PROJ_SUBTILES
    for t in range(PROJ_SUBTILES):
        rows = slice(t * sub, (t + 1) * sub)
        h = (_rms_scale(x_ref[rows, :]) * g_ref[...]).astype(BF16)
        for pair in range(PROJ_W // (2 * LANE_TILE)):
            proj = _dot(h, w_ref[:, 2 * pair * LANE_TILE:2 * (pair + 1) * LANE_TILE])
            for half in range(2):
                c = 2 * pair + half
                chunk = proj[:, half * LANE_TILE:(half + 1) * LANE_TILE]
                if c * LANE_TILE < NORMED_W:
                    chunk = _head_norm(chunk, ones, gain_ref[:, c * LANE_TILE:(c + 1) * LANE_TILE])
                qkv_ref[rows, offs[c]:offs[c] + LANE_TILE] = chunk.astype(BF16)


def _in_proj(x2d, g_mix_l, w_in_l, gain_l, ones):
    tokens = x2d.shape[0]
    tm = PROJ_TILE
    assert tokens % tm == 0
    const = lambda i: (0, 0)
    return pl.pallas_call(
        _in_proj_kernel,
        grid=(tokens // tm,),
        in_specs=[
            pl.BlockSpec((tm, D_MODEL), lambda i: (i, 0)),
            pl.BlockSpec((1, D_MODEL), const),
            pl.BlockSpec((D_MODEL, PROJ_W), const),
            pl.BlockSpec((1, NORMED_W), const),
            pl.BlockSpec((LANE_TILE, LANE_TILE), const),
        ],
        out_specs=pl.BlockSpec((tm, PROJ_W), lambda i: (i, 0)),
        out_shape=jax.ShapeDtypeStruct((tokens, PROJ_W), BF16),
        compiler_params=pltpu.CompilerParams(
            dimension_semantics=("arbitrary",), vmem_limit_bytes=VMEM_LIMIT_BYTES),
        name="in_proj",
    )(x2d, g_mix_l, w_in_l, gain_l, ones)


def _attn_kernel(seq, tq,
                 x_ref, q_ref, prev_ref, next_ref, kvm_ref,
                 wbias_ref, nat_ref, sink_ref, og_ref, wout_ref, out_ref,
                 kw_buf, vw_buf, kb_buf):
    i = pl.program_id(1)
    n_blocks = seq // A_BLOCK
    n_rows = seq // GRID_W
    blocks_per_tile = tq // A_BLOCK
    rows_per_tile = tq // GRID_W
    o_buf = out_ref

    lo_b = _lane_mask(128, 0, HEAD_DIM, BF16)
    hi_b = _lane_mask(128, HEAD_DIM, 128, BF16)
    pieces = ((prev_ref, slice(NAT_HALO - A_BLOCK, NAT_HALO), 0, 0, A_BLOCK),
              (q_ref, slice(0, tq), KVA_OFF, A_BLOCK, tq),
              (next_ref, slice(0, A_BLOCK), 0, A_BLOCK + tq, A_BLOCK))
    for src, src_rows, lane0, r0, n in pieces:
        rows = slice(r0, r0 + n)
        for g in range(A_KV_HEADS):
            k2 = src[src_rows, lane0 + 128 * g:lane0 + 128 * (g + 1)]
            v2 = src[src_rows, lane0 + 256 + 128 * g:lane0 + 256 + 128 * (g + 1)]
            kw_buf[rows, 256 * g:256 * g + 128] = k2 * lo_b
            kw_buf[rows, 256 * g + 128:256 * (g + 1)] = k2 * hi_b
            vw_buf[rows, 512 * g:512 * g + 128] = v2 * lo_b
            vw_buf[rows, 512 * g + 128:512 * g + 256] = jnp.broadcast_to(_lane_mask(128, 0, 1, BF16), (n, 128))
            vw_buf[rows, 512 * g + 256:512 * g + 384] = v2 * hi_b
            vw_buf[rows, 512 * g + 384:512 * (g + 1)] = jnp.broadcast_to(_lane_mask(128, 1, 2, BF16), (n, 128))
    kb_buf[0:NAT_HALO] = prev_ref[:, KV_W:2 * KV_W]
    kb_buf[NAT_HALO:NAT_HALO + tq] = q_ref[:, KVB_OFF:KVB_OFF + KV_W]
    kb_buf[NAT_HALO + tq:2 * NAT_HALO + tq] = next_ref[:, KV_W:2 * KV_W]

    top_rows = lax.broadcasted_iota(jnp.int32, (2 * A_BLOCK, 1), 0) < A_BLOCK
    low_lanes = lax.broadcasted_iota(jnp.int32, (1, 128), 1) < HEAD_DIM

    band_col = lax.broadcasted_iota(jnp.int32, (1, 6 * A_BLOCK), 1)
    band_col = jnp.where(band_col >= 3 * A_BLOCK, band_col - 3 * A_BLOCK, band_col)
    no_leading_keys = jnp.where(band_col < A_BLOCK, -MASKED, 0.0)
    no_trailing_keys = jnp.where(band_col >= 2 * A_BLOCK, -MASKED, 0.0)

    def window_block(j, carry):
        blk = i * blocks_per_tile + j
        edge = jnp.where(blk == 0, no_leading_keys, jnp.where(blk == n_blocks - 1, no_trailing_keys, 0.0))
        q0 = pl.multiple_of(j * A_BLOCK, A_BLOCK)
        band = pl.ds(q0, 3 * A_BLOCK)
        for g in range(A_KV_HEADS):
            qg = q_ref[pl.ds(q0, A_BLOCK), 256 * g:256 * (g + 1)]
            lhs = jnp.concatenate([qg[:, 0:128], qg[:, 128:256]], axis=0)
            keys = jnp.concatenate(
                [kw_buf[band, 256 * g:256 * g + 128], kw_buf[band, 256 * g + 128:256 * (g + 1)]], axis=0)
            s = _dot_nt(lhs, keys) + wbias_ref[g] + edge
            sink_lo = jnp.where(top_rows, sink_ref[4 * g], sink_ref[4 * g + 2])
            sink_hi = jnp.where(top_rows, sink_ref[4 * g + 1], sink_ref[4 * g + 3])
            s_lo = s[:, 0:3 * A_BLOCK]
            s_hi = s[:, 3 * A_BLOCK:6 * A_BLOCK]
            m_lo = jnp.max(s_lo, axis=-1, keepdims=True)
            m_hi = jnp.max(s_hi, axis=-1, keepdims=True)
            p = jnp.concatenate([jnp.exp2(s_lo - m_lo), jnp.exp2(s_hi - m_hi)], axis=1).astype(BF16)
            vals = jnp.concatenate(
                [vw_buf[band, 512 * g:512 * g + 256], vw_buf[band, 512 * g + 256:512 * (g + 1)]], axis=0)
            o = _dot(p, vals)
            l_lo = o[:, 128:129] + jnp.exp2(sink_lo - m_lo)
            l_hi = o[:, 129:130] + jnp.exp2(sink_hi - m_hi)
            res = o[:, 0:128] / jnp.where(low_lanes, l_lo, l_hi)
            o_buf[pl.ds(q0, A_BLOCK), 256 * g:256 * g + 128] = res[0:A_BLOCK]
            o_buf[pl.ds(q0, A_BLOCK), 256 * g + 128:256 * (g + 1)] = res[A_BLOCK:2 * A_BLOCK]
        return carry

    lax.fori_loop(0, blocks_per_tile, window_block, 0, unroll=blocks_per_tile)

    head_b = [_lane_mask(B_W, HEAD_DIM * h, HEAD_DIM * (h + 1), BF16) for h in range(B_HEADS)]

    def nat_scores(rr):
        r = i * rows_per_tile + rr
        start = jnp.clip(r - NB_ROWS // 2, 0, n_rows - NB_ROWS)
        d0 = (NB_ROWS - 1) - (r - start)
        k0 = pl.multiple_of((start - i * rows_per_tile + NB_ROWS) * GRID_W, GRID_W)
        kwin = kb_buf[pl.ds(k0, NAT_HALO), 0:B_W]
        qrow = q_ref[rr * GRID_W:(rr + 1) * GRID_W, A_Q:A_Q + B_W]
        lhs = jnp.concatenate([qrow * head_b[h] for h in range(B_HEADS)], axis=0)
        bias = jnp.concatenate(
            [jnp.concatenate([nat_ref[h, d0 + 2 * m] for m in range(NB_ROWS // 2)], axis=1)
             for h in range(B_HEADS)], axis=0)
        return _dot_nt(lhs, kwin) + bias, k0

    def nat_finish(rr, s, k0, bounded):
        vwin = kb_buf[pl.ds(k0, NAT_HALO), B_W:2 * B_W]
        p = jnp.exp2(s) if bounded else jnp.exp2(s - jnp.max(s, axis=-1, keepdims=True))
        l = jnp.sum(p, axis=-1, keepdims=True)
        pv = _dot(p.astype(BF16), vwin)
        rows = [slice(GRID_W * h, GRID_W * (h + 1)) for h in range(B_HEADS)]
        o_buf[rr * GRID_W:(rr + 1) * GRID_W, A_Q:A_Q + B_W] = (
            _by_head([pv[r] for r in rows]) / _by_head([l[r] for r in rows]))
        return l

    def neighbourhood_and_memory(bounded):
        pending = nat_scores(0)
        l_min = None
        for rr in range(1, rows_per_tile + 1):
            ahead = nat_scores(rr) if rr < rows_per_tile else None
            l = nat_finish(rr - 1, *pending, bounded)
            l_min = l if l_min is None else jnp.minimum(l_min, l)
            pending = ahead

        km = kvm_ref[:, 0:M_W]
        vm = kvm_ref[:, M_W:2 * M_W]
        qm = q_ref[:, A_Q + B_W:Q_W]
        s = _dot_nt(qm, jnp.concatenate([km * head_b[h] for h in range(M_HEADS)], axis=0))
        probs, sums = [], []
        for h in range(M_HEADS):
            s_h = s[:, N_MEM * h:N_MEM * (h + 1)]
            shift = sink_ref[MEM_BOUND_SLOT] if bounded else jnp.max(s_h, axis=-1, keepdims=True)
            p = jnp.exp2(s_h - shift)
            sums.append(jnp.sum(p, axis=-1, keepdims=True))
            probs.append(p.astype(BF16))
        om = _dot(jnp.concatenate(probs, axis=1),
                  jnp.concatenate([vm * head_b[h] for h in range(M_HEADS)], axis=0))
        o_buf[:, A_Q + B_W:Q_W] = om / _by_head(sums)
        m_min = jnp.minimum(jnp.minimum(sums[0], sums[1]), jnp.minimum(sums[2], sums[3]))
        return jnp.minimum(jnp.min(l_min), jnp.min(m_min))

    smallest = neighbourhood_and_memory(True)

    @pl.when(smallest < MIN_DENOMINATOR)
    def _():
        neighbourhood_and_memory(False)

    o = jnp.concatenate(
        [_rms_scale(o_buf[:, 0:A_Q]), _rms_scale(o_buf[:, A_Q:A_Q + B_W]), _rms_scale(o_buf[:, A_Q + B_W:Q_W])],
        axis=-1) * og_ref[...]
    out_ref[...] = x_ref[...] + _dot(o.astype(BF16), wout_ref[...])


def _attention(x, qkv, kvm_l, wbias_tab, nat_tab, sink_l, og_l, wout_l):
    batch, seq, _ = x.shape
    tq = ATTN_TILE
    assert seq % tq == 0 and tq % NAT_HALO == 0 and seq >= 2 * A_BLOCK and seq >= NB_ROWS * GRID_W
    nt = seq // tq
    halo_per_tile = tq // NAT_HALO
    n_halo_blocks = seq // NAT_HALO
    const2 = lambda b, i: (0, 0)
    const4 = lambda b, i: (0, 0, 0, 0)
    once = pl.Buffered(1)
    return pl.pallas_call(
        functools.partial(_attn_kernel, seq, tq),
        grid=(batch, nt),
        in_specs=[
            pl.BlockSpec((None, tq, D_MODEL), lambda b, i: (b, i, 0)),
            pl.BlockSpec((None, tq, PROJ_W), lambda b, i: (b, i, 0)),
            pl.BlockSpec((None, NAT_HALO, 2 * KV_W),
                         lambda b, i: (b, jnp.maximum(i * halo_per_tile - 1, 0), 1)),
            pl.BlockSpec((None, NAT_HALO, 2 * KV_W),
                         lambda b, i: (b, jnp.minimum((i + 1) * halo_per_tile, n_halo_blocks - 1), 1)),
            pl.BlockSpec((None, N_MEM, 2 * M_W), lambda b, i: (b, 0, 0)),
            pl.BlockSpec((A_KV_HEADS, 2 * A_BLOCK, 6 * A_BLOCK), lambda b, i: (0, 0, 0), pipeline_mode=once),
            pl.BlockSpec((B_HEADS, 2 * NB_ROWS - 2, GRID_W, 2 * GRID_W), const4, pipeline_mode=once),
            pl.BlockSpec(memory_space=pltpu.SMEM),
            pl.BlockSpec((1, Q_W), const2),
            pl.BlockSpec((Q_W, D_MODEL), const2, pipeline_mode=once),
        ],
        out_specs=pl.BlockSpec((None, tq, D_MODEL), lambda b, i: (b, i, 0)),
        out_shape=jax.ShapeDtypeStruct((batch, seq, D_MODEL), F32),
        scratch_shapes=[
            pltpu.VMEM((tq + 2 * A_BLOCK, 2 * 256), BF16),
            pltpu.VMEM((tq + 2 * A_BLOCK, 2 * 512), BF16),
            pltpu.VMEM((tq + 2 * NAT_HALO, KV_W), BF16),
        ],
        compiler_params=pltpu.CompilerParams(
            dimension_semantics=("arbitrary", "arbitrary"), vmem_limit_bytes=VMEM_LIMIT_BYTES),
        name="attention",
    )(x, qkv, qkv, qkv, kvm_l, wbias_tab, nat_tab, sink_l, og_l, wout_l)


def _ffn_kernel(x_ref, g_ref, w1_ref, w2_ref, out_ref):
    x = x_ref[...]
    h = (_rms_scale(x) * g_ref[...]).astype(BF16)
    acc = x
    for c in range(D_FF // FF_CHUNK):
        f = _dot(h, w1_ref[:, c * FF_CHUNK:(c + 1) * FF_CHUNK])
        a = jnp.square(jnp.maximum(f, 0.0)).astype(BF16)
        acc = acc + _dot(a, w2_ref[c * FF_CHUNK:(c + 1) * FF_CHUNK, :])
    out_ref[...] = acc


def _ffn(x2d, g_ff_l, w1_l, w2_l):
    tokens = x2d.shape[0]
    tm = FFN_TILE
    assert tokens % tm == 0
    const = lambda i: (0, 0)
    return pl.pallas_call(
        _ffn_kernel,
        grid=(tokens // tm,),
        in_specs=[
            pl.BlockSpec((tm, D_MODEL), lambda i: (i, 0)),
            pl.BlockSpec((1, D_MODEL), const),
            pl.BlockSpec((D_MODEL, D_FF), const, pipeline_mode=pl.Buffered(1)),
            pl.BlockSpec((D_FF, D_MODEL), const, pipeline_mode=pl.Buffered(1)),
        ],
        out_specs=pl.BlockSpec((tm, D_MODEL), lambda i: (i, 0)),
        out_shape=jax.ShapeDtypeStruct((tokens, D_MODEL), F32),
        compiler_params=pltpu.CompilerParams(
            dimension_semantics=("arbitrary",), vmem_limit_bytes=VMEM_LIMIT_BYTES),
        name="ffn",
    )(x2d, g_ff_l, w1_l, w2_l)


def _window_bias_table():
    qi = np.arange(A_BLOCK)[:, None]
    si = np.arange(3 * A_BLOCK)[None, :]
    dist = np.abs(si - A_BLOCK - qi).astype(np.float32)
    slopes = (2.0 ** (-8.0 * np.arange(1, A_Q_HEADS + 1) / A_Q_HEADS)).astype(np.float32)
    tab = np.empty((A_KV_HEADS, 2 * A_BLOCK, 6 * A_BLOCK), np.float32)
    for g in range(A_KV_HEADS):
        for hh in range(4):
            rows = slice(A_BLOCK * (hh // 2), A_BLOCK * (hh // 2 + 1))
            cols = slice(3 * A_BLOCK * (hh % 2), 3 * A_BLOCK * (hh % 2 + 1))
            tab[g, rows, cols] = np.where(dist <= WINDOW, -(slopes[4 * g + hh] * dist) * LOG2E, -MASKED)
    return jnp.asarray(tab)


def _score_bound(gain_q, gain_k):
    return 1.02 * HEAD_DIM * (HEAD_DIM ** -0.5 * LOG2E) * jnp.max(jnp.abs(gain_q)) * jnp.max(jnp.abs(gain_k))


def _scalar_table(sink_l, qk_gain_l):
    g = qk_gain_l.astype(F32)
    return jnp.concatenate([sink_l.astype(F32) * LOG2E, _score_bound(g[4], g[5])[None],
                            jnp.zeros((16 - A_Q_HEADS - 1,), F32)])


def _neighbourhood_bias_table(rpb_l, qk_gain_l):
    c = np.arange(GRID_W)
    dc = np.clip(c[None, :] - c[:, None] + (NB_COLS - 1), 0, 2 * NB_COLS - 2)
    onehot = jnp.asarray((dc[None] == np.arange(2 * NB_COLS - 1)[:, None, None]).astype(np.float32))
    t = jnp.einsum("hdc,cqk->hdqk", rpb_l.astype(F32), onehot, precision=lax.Precision.HIGHEST) * LOG2E
    cstart = np.clip(c - NB_COLS // 2, 0, GRID_W - NB_COLS)
    cvalid = (c[None, :] >= cstart[:, None]) & (c[None, :] < cstart[:, None] + NB_COLS)
    g = qk_gain_l.astype(F32)
    bound = _score_bound(g[2], g[3]) + jnp.max(rpb_l.astype(F32), axis=(1, 2)) * LOG2E
    t = jnp.where(cvalid[None, None], t - bound[:, None, None, None], -MASKED)
    return jnp.concatenate([t[:, :-1], t[:, 1:]], axis=-1)


def _pack_in_proj(w_in_l):
    qa, ka, va, qb, kb, vb, qm = jnp.split(
        w_in_l, [A_Q, A_Q + A_KV, A_Q + 2 * A_KV, A_Q + 2 * A_KV + B_W,
                 A_Q + 2 * A_KV + 2 * B_W, A_Q + 2 * A_KV + 3 * B_W], axis=1)
    twice = lambda t: jnp.concatenate(
        [t[:, :HEAD_DIM], t[:, :HEAD_DIM], t[:, HEAD_DIM:], t[:, HEAD_DIM:]], axis=1)
    return jnp.concatenate([qa, qb, qm, twice(ka), kb, twice(va), vb], axis=1).astype(BF16)


def _pack_qk_gain(qk_gain_l):
    scale = HEAD_DIM ** -0.5 * LOG2E
    g = qk_gain_l.astype(F32)
    return jnp.concatenate([
        jnp.tile(g[0], A_Q_HEADS) * scale, jnp.tile(g[2], B_HEADS) * scale, jnp.tile(g[4], M_HEADS) * scale,
        jnp.tile(g[1], 2 * A_KV_HEADS), jnp.tile(g[3], B_HEADS)])[None, :]


def _trunk(x, kvm, params):
    batch, seq, _ = x.shape
    for l in range(DEPTH):
        p = params[l]
        qkv = _in_proj(x.reshape(batch * seq, D_MODEL), p["g_mix"], p["w_in"], p["qk_gain"], params["ones"])
        x = _attention(x, qkv.reshape(batch, seq, PROJ_W),
                       kvm[l], params["wbias"], p["nat"], p["sink"], p["o_gain"], p["w_out"])
        x = _ffn(x.reshape(batch * seq, D_MODEL), p["g_ff"], p["w_ff1"], p["w_ff2"]).reshape(batch, seq, D_MODEL)
    return x


def kernel(x_prompt, x_sample, mem_prompt, mem_sample, g_mix, w_in, qk_gain, sink, rpb,
           o_gain, w_out, g_mem, w_mem_kv, g_ff, w_ff1, w_ff2):
    ones = jnp.asarray(
        np.kron(np.eye(LANE_TILE // HEAD_DIM), np.full((HEAD_DIM, HEAD_DIM), 1.0 / HEAD_DIM)), BF16)
    params = {"ones": ones, "wbias": _window_bias_table()}
    for l in range(DEPTH):
        params[l] = {
            "g_mix": g_mix[l].astype(F32)[None, :],
            "w_in": _pack_in_proj(w_in[l]),
            "qk_gain": _pack_qk_gain(qk_gain[l]),
            "nat": _neighbourhood_bias_table(rpb[l], qk_gain[l]),
            "sink": _scalar_table(sink[l], qk_gain[l]),
            "o_gain": o_gain[l].astype(F32)[None, :],
            "w_out": w_out[l].astype(BF16),
            "g_ff": g_ff[l].astype(F32)[None, :],
            "w_ff1": w_ff1[l].astype(BF16),
            "w_ff2": w_ff2[l].astype(BF16),
        }
    km_gain = jnp.tile(qk_gain[:, 5].astype(F32), (1, M_HEADS))[:, None, :]
    g_mem3 = g_mem.astype(F32)[:, None, :]
    w_mem = w_mem_kv.astype(BF16)
    kvm_prompt = _mem_kv(mem_prompt, g_mem3, w_mem, km_gain, ones)
    kvm_sample = _mem_kv(mem_sample, g_mem3, w_mem, km_gain, ones)
    return (_trunk(x_prompt, kvm_prompt, params), _trunk(x_sample, kvm_sample, params))
```

```python
import functools

import numpy as np
import jax
import jax.numpy as jnp
from jax import lax
from jax.experimental import pallas as pl
from jax.experimental.pallas import tpu as pltpu

D_MODEL = 1024
DEPTH = 2
HEAD_DIM = 64
A_Q_HEADS = 8
A_KV_HEADS = 2
WINDOW = 128
A_BLOCK = 128
B_HEADS = 4
GRID_W = 64
NB_ROWS = 8
NB_COLS = 16
M_HEADS = 4
N_MEM = 256
D_FF = 4 * D_MODEL
EPS = 1e-6

A_Q = A_Q_HEADS * HEAD_DIM
A_KV = A_KV_HEADS * HEAD_DIM
B_W = B_HEADS * HEAD_DIM
M_W = M_HEADS * HEAD_DIM

Q_W = A_Q + B_W + M_W
NORMED_W = Q_W + 2 * 256
PROJ_W = NORMED_W + 2 * 256
LANE_TILE = 256
KV_W = 512
KVA_OFF = Q_W
KVB_OFF = Q_W + KV_W

NAT_HALO = NB_ROWS * GRID_W
MASKED = 1e30
MEM_BOUND_SLOT = A_Q_HEADS
MIN_DENOMINATOR = 2.0 ** -100
LOG2E = float(np.log2(np.e))

PROJ_TILE = 1024
PROJ_SUBTILES = 2
ATTN_TILE = 1024
FFN_TILE = 1024
FF_CHUNK = 1024
MEM_BATCH = 4
VMEM_LIMIT_BYTES = 58 * 1024 * 1024

F32 = jnp.float32
BF16 = jnp.bfloat16


def _dot(a, b):
    return jnp.dot(a, b, preferred_element_type=F32)


def _dot_nt(a, b):
    return lax.dot_general(a, b, (((1,), (1,)), ((), ())), preferred_element_type=F32)


def _rms_scale(x):
    return x * lax.rsqrt(jnp.mean(x * x, axis=-1, keepdims=True) + EPS)


def _head_norm(p, ones, gain):
    ms = _dot((p * p).astype(BF16), ones)
    return p * lax.rsqrt(ms + EPS) * gain


def _lane_mask(width, lo, hi, dtype):
    lane = lax.broadcasted_iota(jnp.int32, (1, width), 1)
    return jnp.where((lane >= lo) & (lane < hi), 1.0, 0.0).astype(dtype)


def _by_head(parts):
    lane = lax.broadcasted_iota(jnp.int32, (1, 4 * HEAD_DIM), 1)
    return jnp.where(lane < 2 * HEAD_DIM,
                     jnp.where(lane < HEAD_DIM, parts[0], parts[1]),
                     jnp.where(lane < 3 * HEAD_DIM, parts[2], parts[3]))


def _mem_kv_kernel(mem_ref, g_ref, w_ref, gain_ref, ones_ref, out_ref):
    nb = mem_ref.shape[0]
    mem = mem_ref[...].reshape(nb * N_MEM, D_MODEL)
    h = (_rms_scale(mem) * g_ref[...]).astype(BF16)
    mkv = _dot(h, w_ref[...])
    km = _head_norm(mkv[:, 0:M_W], ones_ref[...], gain_ref[...])
    out_ref[:, :, 0:M_W] = km.astype(BF16).reshape(nb, N_MEM, M_W)
    out_ref[:, :, M_W:2 * M_W] = mkv[:, M_W:2 * M_W].astype(BF16).reshape(nb, N_MEM, M_W)


def _mem_kv(mem, g_mem, w_mem_kv, km_gain, ones):
    batch = mem.shape[0]
    nb = MEM_BATCH
    assert batch % nb == 0
    return pl.pallas_call(
        _mem_kv_kernel,
        grid=(DEPTH, batch // nb),
        in_specs=[
            pl.BlockSpec((nb, N_MEM, D_MODEL), lambda l, b: (b, 0, 0)),
            pl.BlockSpec((None, 1, D_MODEL), lambda l, b: (l, 0, 0)),
            pl.BlockSpec((None, D_MODEL, 2 * M_W), lambda l, b: (l, 0, 0)),
            pl.BlockSpec((None, 1, M_W), lambda l, b: (l, 0, 0)),
            pl.BlockSpec((LANE_TILE, LANE_TILE), lambda l, b: (0, 0)),
        ],
        out_specs=pl.BlockSpec((None, nb, N_MEM, 2 * M_W), lambda l, b: (l, b, 0, 0)),
        out_shape=jax.ShapeDtypeStruct((DEPTH, batch, N_MEM, 2 * M_W), BF16),
        compiler_params=pltpu.CompilerParams(
            dimension_semantics=("arbitrary", "arbitrary"), vmem_limit_bytes=VMEM_LIMIT_BYTES),
        name="mem_kv",
    )(mem, g_mem, w_mem_kv, km_gain, ones)


def _in_proj_kernel(x_ref, g_ref, w_ref, gain_ref, ones_ref, qkv_ref):
    ones = ones_ref[...]
    offs = (0, 256, 512, 768, KVA_OFF, KVB_OFF, KVA_OFF + 256, KVB_OFF + 256)
    sub = x_ref.shape[0] // PROJ_SUBTILES
    for t in range(PROJ_SUBTILES):
        rows = slice(t * sub, (t + 1) * sub)
        h = (_rms_scale(x_ref[rows, :]) * g_ref[...]).astype(BF16)
        for pair in range(PROJ_W // (2 * LANE_TILE)):
            proj = _dot(h, w_ref[:, 2 * pair * LANE_TILE:2 * (pair + 1) * LANE_TILE])
            for half in range(2):
                c = 2 * pair + half
                chunk = proj[:, half * LANE_TILE:(half + 1) * LANE_TILE]
                if c * LANE_TILE < NORMED_W:
                    chunk = _head_norm(chunk, ones, gain_ref[:, c * LANE_TILE:(c + 1) * LANE_TILE])
                qkv_ref[rows, offs[c]:offs[c] + LANE_TILE] = chunk.astype(BF16)


def _in_proj(x2d, g_mix_l, w_in_all, layer, gain_l, ones):
    tokens = x2d.shape[0]
    tm = PROJ_TILE
    assert tokens % tm == 0
    const = lambda i: (0, 0)
    return pl.pallas_call(
        _in_proj_kernel,
        grid=(tokens // tm,),
        in_specs=[
            pl.BlockSpec((tm, D_MODEL), lambda i: (i, 0)),
            pl.BlockSpec((1, D_MODEL), const),
            pl.BlockSpec((None, D_MODEL, PROJ_W), lambda i: (layer, 0, 0)),
            pl.BlockSpec((1, NORMED_W), const),
            pl.BlockSpec((LANE_TILE, LANE_TILE), const),
        ],
        out_specs=pl.BlockSpec((tm, PROJ_W), lambda i: (i, 0)),
        out_shape=jax.ShapeDtypeStruct((tokens, PROJ_W), BF16),
        compiler_params=pltpu.CompilerParams(
            dimension_semantics=("arbitrary",), vmem_limit_bytes=VMEM_LIMIT_BYTES),
        name="in_proj",
    )(x2d, g_mix_l, w_in_all, gain_l, ones)


def _attn_kernel(seq, tq,
                 x_ref, q_ref, prev_ref, next_ref, kvm_ref,
                 wbias_ref, nat_ref, sink_ref, og_ref, wout_ref, out_ref,
                 kw_buf, vw_buf, kb_buf):
    i = pl.program_id(1)
    n_blocks = seq // A_BLOCK
    n_rows = seq // GRID_W
    blocks_per_tile = tq // A_BLOCK
    rows_per_tile = tq // GRID_W
    o_buf = out_ref

    lo_b = _lane_mask(128, 0, HEAD_DIM, BF16)
    hi_b = _lane_mask(128, HEAD_DIM, 128, BF16)
    pieces = ((prev_ref, slice(NAT_HALO - A_BLOCK, NAT_HALO), 0, 0, A_BLOCK),
              (q_ref, slice(0, tq), KVA_OFF, A_BLOCK, tq),
              (next_ref, slice(0, A_BLOCK), 0, A_BLOCK + tq, A_BLOCK))
    for src, src_rows, lane0, r0, n in pieces:
        rows = slice(r0, r0 + n)
        for g in range(A_KV_HEADS):
            k2 = src[src_rows, lane0 + 128 * g:lane0 + 128 * (g + 1)]
            v2 = src[src_rows, lane0 + 256 + 128 * g:lane0 + 256 + 128 * (g + 1)]
            kw_buf[rows, 256 * g:256 * g + 128] = k2 * lo_b
            kw_buf[rows, 256 * g + 128:256 * (g + 1)] = k2 * hi_b
            vw_buf[rows, 512 * g:512 * g + 128] = v2 * lo_b
            vw_buf[rows, 512 * g + 128:512 * g + 256] = jnp.broadcast_to(_lane_mask(128, 0, 1, BF16), (n, 128))
            vw_buf[rows, 512 * g + 256:512 * g + 384] = v2 * hi_b
            vw_buf[rows, 512 * g + 384:512 * (g + 1)] = jnp.broadcast_to(_lane_mask(128, 1, 2, BF16), (n, 128))
    kb_buf[0:NAT_HALO] = prev_ref[:, KV_W:2 * KV_W]
    kb_buf[NAT_HALO:NAT_HALO + tq] = q_ref[:, KVB_OFF:KVB_OFF + KV_W]
    kb_buf[NAT_HALO + tq:2 * NAT_HALO + tq] = next_ref[:, KV_W:2 * KV_W]

    top_rows = lax.broadcasted_iota(jnp.int32, (2 * A_BLOCK, 1), 0) < A_BLOCK
    low_lanes = lax.broadcasted_iota(jnp.int32, (1, 128), 1) < HEAD_DIM

    band_col = lax.broadcasted_iota(jnp.int32, (1, 6 * A_BLOCK), 1)
    band_col = jnp.where(band_col >= 3 * A_BLOCK, band_col - 3 * A_BLOCK, band_col)
    no_leading_keys = jnp.where(band_col < A_BLOCK, -MASKED, 0.0)
    no_trailing_keys = jnp.where(band_col >= 2 * A_BLOCK, -MASKED, 0.0)

    def window_block(j, carry):
        blk = i * blocks_per_tile + j
        edge = jnp.where(blk == 0, no_leading_keys, jnp.where(blk == n_blocks - 1, no_trailing_keys, 0.0))
        q0 = pl.multiple_of(j * A_BLOCK, A_BLOCK)
        band = pl.ds(q0, 3 * A_BLOCK)
        for g in range(A_KV_HEADS):
            qg = q_ref[pl.ds(q0, A_BLOCK), 256 * g:256 * (g + 1)]
            lhs = jnp.concatenate([qg[:, 0:128], qg[:, 128:256]], axis=0)
            keys = jnp.concatenate(
                [kw_buf[band, 256 * g:256 * g + 128], kw_buf[band, 256 * g + 128:256 * (g + 1)]], axis=0)
            s = _dot_nt(lhs, keys) + wbias_ref[g] + edge
            sink_lo = jnp.where(top_rows, sink_ref[4 * g], sink_ref[4 * g + 2])
            sink_hi = jnp.where(top_rows, sink_ref[4 * g + 1], sink_ref[4 * g + 3])
            s_lo = s[:, 0:3 * A_BLOCK]
            s_hi = s[:, 3 * A_BLOCK:6 * A_BLOCK]
            m_lo = jnp.max(s_lo, axis=-1, keepdims=True)
            m_hi = jnp.max(s_hi, axis=-1, keepdims=True)
            p = jnp.concatenate([jnp.exp2(s_lo - m_lo), jnp.exp2(s_hi - m_hi)], axis=1).astype(BF16)
            vals = jnp.concatenate(
                [vw_buf[band, 512 * g:512 * g + 256], vw_buf[band, 512 * g + 256:512 * (g + 1)]], axis=0)
            o = _dot(p, vals)
            l_lo = o[:, 128:129] + jnp.exp2(sink_lo - m_lo)
            l_hi = o[:, 129:130] + jnp.exp2(sink_hi - m_hi)
            res = o[:, 0:128] / jnp.where(low_lanes, l_lo, l_hi)
            o_buf[pl.ds(q0, A_BLOCK), 256 * g:256 * g + 128] = res[0:A_BLOCK]
            o_buf[pl.ds(q0, A_BLOCK), 256 * g + 128:256 * (g + 1)] = res[A_BLOCK:2 * A_BLOCK]
        return carry

    lax.fori_loop(0, blocks_per_tile, window_block, 0, unroll=blocks_per_tile)

    head_b = [_lane_mask(B_W, HEAD_DIM * h, HEAD_DIM * (h + 1), BF16) for h in range(B_HEADS)]

    def nat_scores(rr):
        r = i * rows_per_tile + rr
        start = jnp.clip(r - NB_ROWS // 2, 0, n_rows - NB_ROWS)
        d0 = (NB_ROWS - 1) - (r - start)
        k0 = pl.multiple_of((start - i * rows_per_tile + NB_ROWS) * GRID_W, GRID_W)
        kwin = kb_buf[pl.ds(k0, NAT_HALO), 0:B_W]
        qrow = q_ref[rr * GRID_W:(rr + 1) * GRID_W, A_Q:A_Q + B_W]
        lhs = jnp.concatenate([qrow * head_b[h] for h in range(B_HEADS)], axis=0)
        bias = jnp.concatenate(
            [jnp.concatenate([nat_ref[h, d0 + 2 * m] for m in range(NB_ROWS // 2)], axis=1)
             for h in range(B_HEADS)], axis=0)
        return _dot_nt(lhs, kwin) + bias, k0

    def nat_finish(rr, s, k0, bounded):
        vwin = kb_buf[pl.ds(k0, NAT_HALO), B_W:2 * B_W]
        p = jnp.exp2(s) if bounded else jnp.exp2(s - jnp.max(s, axis=-1, keepdims=True))
        l = jnp.sum(p, axis=-1, keepdims=True)
        pv = _dot(p.astype(BF16), vwin)
        rows = [slice(GRID_W * h, GRID_W * (h + 1)) for h in range(B_HEADS)]
        o_buf[rr * GRID_W:(rr + 1) * GRID_W, A_Q:A_Q + B_W] = (
            _by_head([pv[r] for r in rows]) / _by_head([l[r] for r in rows]))
        return l

    def neighbourhood_and_memory(bounded):
        pending = nat_scores(0)
        l_min = None
        for rr in range(1, rows_per_tile + 1):
            ahead = nat_scores(rr) if rr < rows_per_tile else None
            l = nat_finish(rr - 1, *pending, bounded)
            l_min = l if l_min is None else jnp.minimum(l_min, l)
            pending = ahead

        km = kvm_ref[:, 0:M_W]
        vm = kvm_ref[:, M_W:2 * M_W]
        qm = q_ref[:, A_Q + B_W:Q_W]
        s = _dot_nt(qm, jnp.concatenate([km * head_b[h] for h in range(M_HEADS)], axis=0))
        probs, sums = [], []
        for h in range(M_HEADS):
            s_h = s[:, N_MEM * h:N_MEM * (h + 1)]
            shift = sink_ref[MEM_BOUND_SLOT] if bounded else jnp.max(s_h, axis=-1, keepdims=True)
            p = jnp.exp2(s_h - shift)
            sums.append(jnp.sum(p, axis=-1, keepdims=True))
            probs.append(p.astype(BF16))
        om = _dot(jnp.concatenate(probs, axis=1),
                  jnp.concatenate([vm * head_b[h] for h in range(M_HEADS)], axis=0))
        o_buf[:, A_Q + B_W:Q_W] = om / _by_head(sums)
        m_min = jnp.minimum(jnp.minimum(sums[0], sums[1]), jnp.minimum(sums[2], sums[3]))
        return jnp.minimum(jnp.min(l_min), jnp.min(m_min))

    smallest = neighbourhood_and_memory(True)

    @pl.when(smallest < MIN_DENOMINATOR)
    def _():
        neighbourhood_and_memory(False)

    o = jnp.concatenate(
        [_rms_scale(o_buf[:, 0:A_Q]), _rms_scale(o_buf[:, A_Q:A_Q + B_W]), _rms_scale(o_buf[:, A_Q + B_W:Q_W])],
        axis=-1) * og_ref[...]
    out_ref[...] = x_ref[...] + _dot(o.astype(BF16), wout_ref[...])


def _attention(x, qkv, kvm_all, wbias_tab, nat_all, layer, sink_l, og_l, wout_all):
    batch, seq, _ = x.shape
    tq = ATTN_TILE
    assert seq % tq == 0 and tq % NAT_HALO == 0 and seq >= 2 * A_BLOCK and seq >= NB_ROWS * GRID_W
    nt = seq // tq
    halo_per_tile = tq // NAT_HALO
    n_halo_blocks = seq // NAT_HALO
    const2 = lambda b, i: (0, 0)
    once = pl.Buffered(1)
    return pl.pallas_call(
        functools.partial(_attn_kernel, seq, tq),
        grid=(batch, nt),
        in_specs=[
            pl.BlockSpec((None, tq, D_MODEL), lambda b, i: (b, i, 0)),
            pl.BlockSpec((None, tq, PROJ_W), lambda b, i: (b, i, 0)),
            pl.BlockSpec((None, NAT_HALO, 2 * KV_W),
                         lambda b, i: (b, jnp.maximum(i * halo_per_tile - 1, 0), 1)),
            pl.BlockSpec((None, NAT_HALO, 2 * KV_W),
                         lambda b, i: (b, jnp.minimum((i + 1) * halo_per_tile, n_halo_blocks - 1), 1)),
            pl.BlockSpec((None, None, N_MEM, 2 * M_W), lambda b, i: (layer, b, 0, 0)),
            pl.BlockSpec((A_KV_HEADS, 2 * A_BLOCK, 6 * A_BLOCK), lambda b, i: (0, 0, 0), pipeline_mode=once),
            pl.BlockSpec((None, B_HEADS, 2 * NB_ROWS - 2, GRID_W, 2 * GRID_W),
                         lambda b, i: (layer, 0, 0, 0, 0), pipeline_mode=once),
            pl.BlockSpec(memory_space=pltpu.SMEM),
            pl.BlockSpec((1, Q_W), const2),
            pl.BlockSpec((None, Q_W, D_MODEL), lambda b, i: (layer, 0, 0), pipeline_mode=once),
        ],
        out_specs=pl.BlockSpec((None, tq, D_MODEL), lambda b, i: (b, i, 0)),
        out_shape=jax.ShapeDtypeStruct((batch, seq, D_MODEL), F32),
        scratch_shapes=[
            pltpu.VMEM((tq + 2 * A_BLOCK, 2 * 256), BF16),
            pltpu.VMEM((tq + 2 * A_BLOCK, 2 * 512), BF16),
            pltpu.VMEM((tq + 2 * NAT_HALO, KV_W), BF16),
        ],
        compiler_params=pltpu.CompilerParams(
            dimension_semantics=("arbitrary", "arbitrary"), vmem_limit_bytes=VMEM_LIMIT_BYTES),
        name="attention",
    )(x, qkv, qkv, qkv, kvm_all, wbias_tab, nat_all, sink_l, og_l, wout_all)


def _ffn_kernel(x_ref, g_ref, w1_ref, w2_ref, out_ref):
    x = x_ref[...]
    h = (_rms_scale(x) * g_ref[...]).astype(BF16)
    acc = x
    for c in range(D_FF // FF_CHUNK):
        f = _dot(h, w1_ref[:, c * FF_CHUNK:(c + 1) * FF_CHUNK])
        a = jnp.square(jnp.maximum(f, 0.0)).astype(BF16)
        acc = acc + _dot(a, w2_ref[c * FF_CHUNK:(c + 1) * FF_CHUNK, :])
    out_ref[...] = acc


def _ffn(x2d, g_ff_l, w1_all, w2_all, layer):
    tokens = x2d.shape[0]
    tm = FFN_TILE
    assert tokens % tm == 0
    const = lambda i: (0, 0)
    return pl.pallas_call(
        _ffn_kernel,
        grid=(tokens // tm,),
        in_specs=[
            pl.BlockSpec((tm, D_MODEL), lambda i: (i, 0)),
            pl.BlockSpec((1, D_MODEL), const),
            pl.BlockSpec((None, D_MODEL, D_FF), lambda i: (layer, 0, 0), pipeline_mode=pl.Buffered(1)),
            pl.BlockSpec((None, D_FF, D_MODEL), lambda i: (layer, 0, 0), pipeline_mode=pl.Buffered(1)),
        ],
        out_specs=pl.BlockSpec((tm, D_MODEL), lambda i: (i, 0)),
        out_shape=jax.ShapeDtypeStruct((tokens, D_MODEL), F32),
        compiler_params=pltpu.CompilerParams(
            dimension_semantics=("arbitrary",), vmem_limit_bytes=VMEM_LIMIT_BYTES),
        name="ffn",
    )(x2d, g_ff_l, w1_all, w2_all)


def _window_bias_table():
    qi = np.arange(A_BLOCK)[:, None]
    si = np.arange(3 * A_BLOCK)[None, :]
    dist = np.abs(si - A_BLOCK - qi).astype(np.float32)
    slopes = (2.0 ** (-8.0 * np.arange(1, A_Q_HEADS + 1) / A_Q_HEADS)).astype(np.float32)
    tab = np.empty((A_KV_HEADS, 2 * A_BLOCK, 6 * A_BLOCK), np.float32)
    for g in range(A_KV_HEADS):
        for hh in range(4):
            rows = slice(A_BLOCK * (hh // 2), A_BLOCK * (hh // 2 + 1))
            cols = slice(3 * A_BLOCK * (hh % 2), 3 * A_BLOCK * (hh % 2 + 1))
            tab[g, rows, cols] = np.where(dist <= WINDOW, -(slopes[4 * g + hh] * dist) * LOG2E, -MASKED)
    return jnp.asarray(tab)


def _score_bound(gain_q, gain_k):
    return 1.02 * HEAD_DIM * (HEAD_DIM ** -0.5 * LOG2E) * jnp.max(jnp.abs(gain_q)) * jnp.max(jnp.abs(gain_k))


def _scalar_table(sink_l, qk_gain_l):
    g = qk_gain_l.astype(F32)
    return jnp.concatenate([sink_l.astype(F32) * LOG2E, _score_bound(g[4], g[5])[None],
                            jnp.zeros((16 - A_Q_HEADS - 1,), F32)])


def _neighbourhood_bias_table(rpb, qk_gain):
    c = np.arange(GRID_W)
    dc = np.clip(c[None, :] - c[:, None] + (NB_COLS - 1), 0, 2 * NB_COLS - 2)
    onehot = jnp.asarray((dc[None] == np.arange(2 * NB_COLS - 1)[:, None, None]).astype(np.float32))
    t = jnp.einsum("lhdc,cqk->lhdqk", rpb.astype(F32), onehot, precision=lax.Precision.HIGHEST) * LOG2E
    cstart = np.clip(c - NB_COLS // 2, 0, GRID_W - NB_COLS)
    cvalid = (c[None, :] >= cstart[:, None]) & (c[None, :] < cstart[:, None] + NB_COLS)
    g = qk_gain.astype(F32)
    qk_bound = jnp.stack([_score_bound(g[l, 2], g[l, 3]) for l in range(DEPTH)])
    bound = qk_bound[:, None] + jnp.max(rpb.astype(F32), axis=(2, 3)) * LOG2E
    t = jnp.where(cvalid, t - bound[:, :, None, None, None], -MASKED)
    return jnp.concatenate([t[:, :, :-1], t[:, :, 1:]], axis=-1)


def _pack_in_proj(w_in):
    qa, ka, va, qb, kb, vb, qm = jnp.split(
        w_in, [A_Q, A_Q + A_KV, A_Q + 2 * A_KV, A_Q + 2 * A_KV + B_W,
               A_Q + 2 * A_KV + 2 * B_W, A_Q + 2 * A_KV + 3 * B_W], axis=-1)
    twice = lambda t: jnp.concatenate(
        [t[..., :HEAD_DIM], t[..., :HEAD_DIM], t[..., HEAD_DIM:], t[..., HEAD_DIM:]], axis=-1)
    return jnp.concatenate([qa, qb, qm, twice(ka), kb, twice(va), vb], axis=-1).astype(BF16)


def _pack_qk_gain(qk_gain_l):
    scale = HEAD_DIM ** -0.5 * LOG2E
    g = qk_gain_l.astype(F32)
    return jnp.concatenate([
        jnp.tile(g[0], A_Q_HEADS) * scale, jnp.tile(g[2], B_HEADS) * scale, jnp.tile(g[4], M_HEADS) * scale,
        jnp.tile(g[1], 2 * A_KV_HEADS), jnp.tile(g[3], B_HEADS)])[None, :]


def _trunk(x, kvm, params):
    batch, seq, _ = x.shape
    for l in range(DEPTH):
        p = params[l]
        qkv = _in_proj(x.reshape(batch * seq, D_MODEL), p["g_mix"], params["w_in"], l, p["qk_gain"], params["ones"])
        x = _attention(x, qkv.reshape(batch, seq, PROJ_W),
                       kvm, params["wbias"], params["nat"], l, p["sink"], p["o_gain"], params["w_out"])
        x = _ffn(x.reshape(batch * seq, D_MODEL), p["g_ff"], params["w_ff1"], params["w_ff2"], l)
        x = x.reshape(batch, seq, D_MODEL)
    return x


def kernel(x_prompt, x_sample, mem_prompt, mem_sample, g_mix, w_in, qk_gain, sink, rpb,
           o_gain, w_out, g_mem, w_mem_kv, g_ff, w_ff1, w_ff2):
    ones = jnp.asarray(
        np.kron(np.eye(LANE_TILE // HEAD_DIM), np.full((HEAD_DIM, HEAD_DIM), 1.0 / HEAD_DIM)), BF16)
    params = {
        "ones": ones, "wbias": _window_bias_table(), "nat": _neighbourhood_bias_table(rpb, qk_gain),
        "w_in": _pack_in_proj(w_in), "w_out": w_out.astype(BF16),
        "w_ff1": w_ff1.astype(BF16), "w_ff2": w_ff2.astype(BF16),
    }
    for l in range(DEPTH):
        params[l] = {
            "g_mix": g_mix[l].astype(F32)[None, :],
            "qk_gain": _pack_qk_gain(qk_gain[l]),
            "sink": _scalar_table(sink[l], qk_gain[l]),
            "o_gain": o_gain[l].astype(F32)[None, :],
            "g_ff": g_ff[l].astype(F32)[None, :],
        }
    km_gain = jnp.tile(qk_gain[:, 5].astype(F32), (1, M_HEADS))[:, None, :]
    g_mem3 = g_mem.astype(F32)[:, None, :]
    w_mem = w_mem_kv.astype(BF16)
    kvm_prompt = _mem_kv(mem_prompt, g_mem3, w_mem, km_gain, ones)
    kvm_sample = _mem_kv(mem_sample, g_mem3, w_mem, km_gain, ones)
    return (_trunk(x_prompt, kvm_prompt, params), _trunk(x_sample, kvm_sample, params))
```

```python
import functools

import numpy as np
import jax
import jax.numpy as jnp
from jax import lax
from jax.experimental import pallas as pl
from jax.experimental.pallas import tpu as pltpu

D_MODEL = 1024
DEPTH = 2
HEAD_DIM = 64
A_Q_HEADS = 8
A_KV_HEADS = 2
WINDOW = 128
A_BLOCK = 128
B_HEADS = 4
GRID_W = 64
NB_ROWS = 8
NB_COLS = 16
M_HEADS = 4
N_MEM = 256
D_FF = 4 * D_MODEL
EPS = 1e-6

A_Q = A_Q_HEADS * HEAD_DIM
A_KV = A_KV_HEADS * HEAD_DIM
B_W = B_HEADS * HEAD_DIM
M_W = M_HEADS * HEAD_DIM

Q_W = A_Q + B_W + M_W
NORMED_W = Q_W + 2 * 256
PROJ_W = NORMED_W + 2 * 256
LANE_TILE = 256
KV_W = 512
KVA_OFF = Q_W
KVB_OFF = Q_W + KV_W

NAT_HALO = NB_ROWS * GRID_W
MASKED = 1e30
MEM_BOUND_SLOT = A_Q_HEADS
MIN_DENOMINATOR = 2.0 ** -100
LOG2E = float(np.log2(np.e))

PROJ_TILE = 1024
PROJ_SUBTILES = 2
ATTN_TILE = 1024
FFN_TILE = 1024
FFN_SUBTILES = 2
FF_CHUNK = 1024
MEM_BATCH = 4
VMEM_LIMIT_BYTES = 58 * 1024 * 1024

F32 = jnp.float32
BF16 = jnp.bfloat16


def _dot(a, b):
    return jnp.dot(a, b, preferred_element_type=F32)


def _dot_nt(a, b):
    return lax.dot_general(a, b, (((1,), (1,)), ((), ())), preferred_element_type=F32)


def _rms_scale(x):
    return x * lax.rsqrt(jnp.mean(x * x, axis=-1, keepdims=True) + EPS)


def _head_norm(p, ones, gain):
    ms = _dot((p * p).astype(BF16), ones)
    return p * lax.rsqrt(ms + EPS) * gain


def _lane_mask(width, lo, hi, dtype):
    lane = lax.broadcasted_iota(jnp.int32, (1, width), 1)
    return jnp.where((lane >= lo) & (lane < hi), 1.0, 0.0).astype(dtype)


def _by_head(parts):
    lane = lax.broadcasted_iota(jnp.int32, (1, 4 * HEAD_DIM), 1)
    return jnp.where(lane < 2 * HEAD_DIM,
                     jnp.where(lane < HEAD_DIM, parts[0], parts[1]),
                     jnp.where(lane < 3 * HEAD_DIM, parts[2], parts[3]))


def _mem_kv_kernel(mem_ref, g_ref, w_ref, gain_ref, ones_ref, out_ref):
    nb = mem_ref.shape[0]
    mem = mem_ref[...].reshape(nb * N_MEM, D_MODEL)
    h = (_rms_scale(mem) * g_ref[...]).astype(BF16)
    mkv = _dot(h, w_ref[...])
    km = _head_norm(mkv[:, 0:M_W], ones_ref[...], gain_ref[...])
    out_ref[:, :, 0:M_W] = km.astype(BF16).reshape(nb, N_MEM, M_W)
    out_ref[:, :, M_W:2 * M_W] = mkv[:, M_W:2 * M_W].astype(BF16).reshape(nb, N_MEM, M_W)


def _mem_kv(mem, g_mem, w_mem_kv, km_gain, ones):
    batch = mem.shape[0]
    nb = MEM_BATCH
    assert batch % nb == 0
    return pl.pallas_call(
        _mem_kv_kernel,
        grid=(DEPTH, batch // nb),
        in_specs=[
            pl.BlockSpec((nb, N_MEM, D_MODEL), lambda l, b: (b, 0, 0)),
            pl.BlockSpec((None, 1, D_MODEL), lambda l, b: (l, 0, 0)),
            pl.BlockSpec((None, D_MODEL, 2 * M_W), lambda l, b: (l, 0, 0)),
            pl.BlockSpec((None, 1, M_W), lambda l, b: (l, 0, 0)),
            pl.BlockSpec((LANE_TILE, LANE_TILE), lambda l, b: (0, 0)),
        ],
        out_specs=pl.BlockSpec((None, nb, N_MEM, 2 * M_W), lambda l, b: (l, b, 0, 0)),
        out_shape=jax.ShapeDtypeStruct((DEPTH, batch, N_MEM, 2 * M_W), BF16),
        compiler_params=pltpu.CompilerParams(
            dimension_semantics=("arbitrary", "arbitrary"), vmem_limit_bytes=VMEM_LIMIT_BYTES),
        name="mem_kv",
    )(mem, g_mem, w_mem_kv, km_gain, ones)


def _in_proj_kernel(x_ref, g_ref, w_ref, gain_ref, ones_ref, qkv_ref):
    ones = ones_ref[...]
    offs = (0, 256, 512, 768, KVA_OFF, KVB_OFF, KVA_OFF + 256, KVB_OFF + 256)
    sub = x_ref.shape[0] // PROJ_SUBTILES
    for t in range(PROJ_SUBTILES):
        rows = slice(t * sub, (t + 1) * sub)
        h = (_rms_scale(x_ref[rows, :]) * g_ref[...]).astype(BF16)
        for pair in range(PROJ_W // (2 * LANE_TILE)):
            proj = _dot(h, w_ref[:, 2 * pair * LANE_TILE:2 * (pair + 1) * LANE_TILE])
            for half in range(2):
                c = 2 * pair + half
                chunk = proj[:, half * LANE_TILE:(half + 1) * LANE_TILE]
                if c * LANE_TILE < NORMED_W:
                    chunk = _head_norm(chunk, ones, gain_ref[:, c * LANE_TILE:(c + 1) * LANE_TILE])
                qkv_ref[rows, offs[c]:offs[c] + LANE_TILE] = chunk.astype(BF16)


def _in_proj(x2d, g_mix_l, w_in_all, layer, gain_l, ones):
    tokens = x2d.shape[0]
    tm = PROJ_TILE
    assert tokens % tm == 0
    const = lambda i: (0, 0)
    return pl.pallas_call(
        _in_proj_kernel,
        grid=(tokens // tm,),
        in_specs=[
            pl.BlockSpec((tm, D_MODEL), lambda i: (i, 0)),
            pl.BlockSpec((1, D_MODEL), const),
            pl.BlockSpec((None, D_MODEL, PROJ_W), lambda i: (layer, 0, 0)),
            pl.BlockSpec((1, NORMED_W), const),
            pl.BlockSpec((LANE_TILE, LANE_TILE), const),
        ],
        out_specs=pl.BlockSpec((tm, PROJ_W), lambda i: (i, 0)),
        out_shape=jax.ShapeDtypeStruct((tokens, PROJ_W), BF16),
        compiler_params=pltpu.CompilerParams(
            dimension_semantics=("arbitrary",), vmem_limit_bytes=VMEM_LIMIT_BYTES),
        name="in_proj",
    )(x2d, g_mix_l, w_in_all, gain_l, ones)


def _attn_kernel(seq, tq,
                 x_ref, q_ref, prev_ref, next_ref, kvm_ref,
                 wbias_ref, nat_ref, sink_ref, og_ref, wout_ref, out_ref,
                 kw_buf, vw_buf, kb_buf):
    i = pl.program_id(1)
    n_blocks = seq // A_BLOCK
    n_rows = seq // GRID_W
    blocks_per_tile = tq // A_BLOCK
    rows_per_tile = tq // GRID_W
    o_buf = out_ref

    lo_b = _lane_mask(128, 0, HEAD_DIM, BF16)
    hi_b = _lane_mask(128, HEAD_DIM, 128, BF16)
    pieces = ((prev_ref, slice(NAT_HALO - A_BLOCK, NAT_HALO), 0, 0, A_BLOCK),
              (q_ref, slice(0, tq), KVA_OFF, A_BLOCK, tq),
              (next_ref, slice(0, A_BLOCK), 0, A_BLOCK + tq, A_BLOCK))
    for src, src_rows, lane0, r0, n in pieces:
        rows = slice(r0, r0 + n)
        for g in range(A_KV_HEADS):
            k2 = src[src_rows, lane0 + 128 * g:lane0 + 128 * (g + 1)]
            v2 = src[src_rows, lane0 + 256 + 128 * g:lane0 + 256 + 128 * (g + 1)]
            kw_buf[rows, 256 * g:256 * g + 128] = k2 * lo_b
            kw_buf[rows, 256 * g + 128:256 * (g + 1)] = k2 * hi_b
            vw_buf[rows, 512 * g:512 * g + 128] = v2 * lo_b
            vw_buf[rows, 512 * g + 128:512 * g + 256] = jnp.broadcast_to(_lane_mask(128, 0, 1, BF16), (n, 128))
            vw_buf[rows, 512 * g + 256:512 * g + 384] = v2 * hi_b
            vw_buf[rows, 512 * g + 384:512 * (g + 1)] = jnp.broadcast_to(_lane_mask(128, 1, 2, BF16), (n, 128))
    kb_buf[0:NAT_HALO] = prev_ref[:, KV_W:2 * KV_W]
    kb_buf[NAT_HALO:NAT_HALO + tq] = q_ref[:, KVB_OFF:KVB_OFF + KV_W]
    kb_buf[NAT_HALO + tq:2 * NAT_HALO + tq] = next_ref[:, KV_W:2 * KV_W]

    top_rows = lax.broadcasted_iota(jnp.int32, (2 * A_BLOCK, 1), 0) < A_BLOCK
    low_lanes = lax.broadcasted_iota(jnp.int32, (1, 128), 1) < HEAD_DIM

    band_col = lax.broadcasted_iota(jnp.int32, (1, 6 * A_BLOCK), 1)
    band_col = jnp.where(band_col >= 3 * A_BLOCK, band_col - 3 * A_BLOCK, band_col)
    no_leading_keys = jnp.where(band_col < A_BLOCK, -MASKED, 0.0)
    no_trailing_keys = jnp.where(band_col >= 2 * A_BLOCK, -MASKED, 0.0)

    def window_block(j, carry):
        blk = i * blocks_per_tile + j
        edge = jnp.where(blk == 0, no_leading_keys, jnp.where(blk == n_blocks - 1, no_trailing_keys, 0.0))
        q0 = pl.multiple_of(j * A_BLOCK, A_BLOCK)
        band = pl.ds(q0, 3 * A_BLOCK)
        for g in range(A_KV_HEADS):
            qg = q_ref[pl.ds(q0, A_BLOCK), 256 * g:256 * (g + 1)]
            lhs = jnp.concatenate([qg[:, 0:128], qg[:, 128:256]], axis=0)
            keys = jnp.concatenate(
                [kw_buf[band, 256 * g:256 * g + 128], kw_buf[band, 256 * g + 128:256 * (g + 1)]], axis=0)
            s = _dot_nt(lhs, keys) + wbias_ref[g] + edge
            sink_lo = jnp.where(top_rows, sink_ref[4 * g], sink_ref[4 * g + 2])
            sink_hi = jnp.where(top_rows, sink_ref[4 * g + 1], sink_ref[4 * g + 3])
            s_lo = s[:, 0:3 * A_BLOCK]
            s_hi = s[:, 3 * A_BLOCK:6 * A_BLOCK]
            m_lo = jnp.max(s_lo, axis=-1, keepdims=True)
            m_hi = jnp.max(s_hi, axis=-1, keepdims=True)
            p = jnp.concatenate([jnp.exp2(s_lo - m_lo), jnp.exp2(s_hi - m_hi)], axis=1).astype(BF16)
            vals = jnp.concatenate(
                [vw_buf[band, 512 * g:512 * g + 256], vw_buf[band, 512 * g + 256:512 * (g + 1)]], axis=0)
            o = _dot(p, vals)
            l_lo = o[:, 128:129] + jnp.exp2(sink_lo - m_lo)
            l_hi = o[:, 129:130] + jnp.exp2(sink_hi - m_hi)
            res = o[:, 0:128] / jnp.where(low_lanes, l_lo, l_hi)
            o_buf[pl.ds(q0, A_BLOCK), 256 * g:256 * g + 128] = res[0:A_BLOCK]
            o_buf[pl.ds(q0, A_BLOCK), 256 * g + 128:256 * (g + 1)] = res[A_BLOCK:2 * A_BLOCK]
        return carry

    lax.fori_loop(0, blocks_per_tile, window_block, 0, unroll=blocks_per_tile)

    head_b = [_lane_mask(B_W, HEAD_DIM * h, HEAD_DIM * (h + 1), BF16) for h in range(B_HEADS)]

    def nat_scores(rr):
        r = i * rows_per_tile + rr
        start = jnp.clip(r - NB_ROWS // 2, 0, n_rows - NB_ROWS)
        d0 = (NB_ROWS - 1) - (r - start)
        k0 = pl.multiple_of((start - i * rows_per_tile + NB_ROWS) * GRID_W, GRID_W)
        kwin = kb_buf[pl.ds(k0, NAT_HALO), 0:B_W]
        qrow = q_ref[rr * GRID_W:(rr + 1) * GRID_W, A_Q:A_Q + B_W]
        lhs = jnp.concatenate([qrow * head_b[h] for h in range(B_HEADS)], axis=0)
        bias = jnp.concatenate(
            [jnp.concatenate([nat_ref[h, d0 + 2 * m] for m in range(NB_ROWS // 2)], axis=1)
             for h in range(B_HEADS)], axis=0)
        return _dot_nt(lhs, kwin) + bias, k0

    def nat_finish(rr, s, k0, bounded):
        vwin = kb_buf[pl.ds(k0, NAT_HALO), B_W:2 * B_W]
        p = jnp.exp2(s) if bounded else jnp.exp2(s - jnp.max(s, axis=-1, keepdims=True))
        l = jnp.sum(p, axis=-1, keepdims=True)
        pv = _dot(p.astype(BF16), vwin)
        rows = [slice(GRID_W * h, GRID_W * (h + 1)) for h in range(B_HEADS)]
        o_buf[rr * GRID_W:(rr + 1) * GRID_W, A_Q:A_Q + B_W] = (
            _by_head([pv[r] for r in rows]) / _by_head([l[r] for r in rows]))
        return l

    def neighbourhood_and_memory(bounded):
        pending = nat_scores(0)
        l_min = None
        for rr in range(1, rows_per_tile + 1):
            ahead = nat_scores(rr) if rr < rows_per_tile else None
            l = nat_finish(rr - 1, *pending, bounded)
            l_min = l if l_min is None else jnp.minimum(l_min, l)
            pending = ahead

        km = kvm_ref[:, 0:M_W]
        vm = kvm_ref[:, M_W:2 * M_W]
        qm = q_ref[:, A_Q + B_W:Q_W]
        s = _dot_nt(qm, jnp.concatenate([km * head_b[h] for h in range(M_HEADS)], axis=0))
        probs, sums = [], []
        for h in range(M_HEADS):
            s_h = s[:, N_MEM * h:N_MEM * (h + 1)]
            shift = sink_ref[MEM_BOUND_SLOT] if bounded else jnp.max(s_h, axis=-1, keepdims=True)
            p = jnp.exp2(s_h - shift)
            sums.append(jnp.sum(p, axis=-1, keepdims=True))
            probs.append(p.astype(BF16))
        om = _dot(jnp.concatenate(probs, axis=1),
                  jnp.concatenate([vm * head_b[h] for h in range(M_HEADS)], axis=0))
        o_buf[:, A_Q + B_W:Q_W] = om / _by_head(sums)
        m_min = jnp.minimum(jnp.minimum(sums[0], sums[1]), jnp.minimum(sums[2], sums[3]))
        return jnp.minimum(jnp.min(l_min), jnp.min(m_min))

    smallest = neighbourhood_and_memory(True)

    @pl.when(smallest < MIN_DENOMINATOR)
    def _():
        neighbourhood_and_memory(False)

    o = jnp.concatenate(
        [_rms_scale(o_buf[:, 0:A_Q]), _rms_scale(o_buf[:, A_Q:A_Q + B_W]), _rms_scale(o_buf[:, A_Q + B_W:Q_W])],
        axis=-1) * og_ref[...]
    out_ref[...] = x_ref[...] + _dot(o.astype(BF16), wout_ref[...])


def _attention(x, qkv, kvm_all, wbias_tab, nat_all, layer, sink_l, og_l, wout_all):
    batch, seq, _ = x.shape
    tq = ATTN_TILE
    assert seq % tq == 0 and tq % NAT_HALO == 0 and seq >= 2 * A_BLOCK and seq >= NB_ROWS * GRID_W
    nt = seq // tq
    halo_per_tile = tq // NAT_HALO
    n_halo_blocks = seq // NAT_HALO
    const2 = lambda b, i: (0, 0)
    once = pl.Buffered(1)
    return pl.pallas_call(
        functools.partial(_attn_kernel, seq, tq),
        grid=(batch, nt),
        in_specs=[
            pl.BlockSpec((None, tq, D_MODEL), lambda b, i: (b, i, 0)),
            pl.BlockSpec((None, tq, PROJ_W), lambda b, i: (b, i, 0)),
            pl.BlockSpec((None, NAT_HALO, 2 * KV_W),
                         lambda b, i: (b, jnp.maximum(i * halo_per_tile - 1, 0), 1)),
            pl.BlockSpec((None, NAT_HALO, 2 * KV_W),
                         lambda b, i: (b, jnp.minimum((i + 1) * halo_per_tile, n_halo_blocks - 1), 1)),
            pl.BlockSpec((None, None, N_MEM, 2 * M_W), lambda b, i: (layer, b, 0, 0)),
            pl.BlockSpec((A_KV_HEADS, 2 * A_BLOCK, 6 * A_BLOCK), lambda b, i: (0, 0, 0), pipeline_mode=once),
            pl.BlockSpec((None, B_HEADS, 2 * NB_ROWS - 2, GRID_W, 2 * GRID_W),
                         lambda b, i: (layer, 0, 0, 0, 0), pipeline_mode=once),
            pl.BlockSpec(memory_space=pltpu.SMEM),
            pl.BlockSpec((1, Q_W), const2),
            pl.BlockSpec((None, Q_W, D_MODEL), lambda b, i: (layer, 0, 0), pipeline_mode=once),
        ],
        out_specs=pl.BlockSpec((None, tq, D_MODEL), lambda b, i: (b, i, 0)),
        out_shape=jax.ShapeDtypeStruct((batch, seq, D_MODEL), F32),
        scratch_shapes=[
            pltpu.VMEM((tq + 2 * A_BLOCK, 2 * 256), BF16),
            pltpu.VMEM((tq + 2 * A_BLOCK, 2 * 512), BF16),
            pltpu.VMEM((tq + 2 * NAT_HALO, KV_W), BF16),
        ],
        compiler_params=pltpu.CompilerParams(
            dimension_semantics=("arbitrary", "arbitrary"), vmem_limit_bytes=VMEM_LIMIT_BYTES),
        name="attention",
    )(x, qkv, qkv, qkv, kvm_all, wbias_tab, nat_all, sink_l, og_l, wout_all)


def _ffn_kernel(x_ref, g_ref, w1_ref, w2_ref, out_ref):
    sub = x_ref.shape[0] // FFN_SUBTILES
    for t in range(FFN_SUBTILES):
        rows = slice(t * sub, (t + 1) * sub)
        x = x_ref[rows, :]
        h = (_rms_scale(x) * g_ref[...]).astype(BF16)
        acc = x
        for c in range(D_FF // FF_CHUNK):
            f = _dot(h, w1_ref[:, c * FF_CHUNK:(c + 1) * FF_CHUNK])
            a = jnp.square(jnp.maximum(f, 0.0)).astype(BF16)
            acc = acc + _dot(a, w2_ref[c * FF_CHUNK:(c + 1) * FF_CHUNK, :])
        out_ref[rows, :] = acc


def _ffn(x2d, g_ff_l, w1_all, w2_all, layer):
    tokens = x2d.shape[0]
    tm = FFN_TILE
    assert tokens % tm == 0
    const = lambda i: (0, 0)
    return pl.pallas_call(
        _ffn_kernel,
        grid=(tokens // tm,),
        in_specs=[
            pl.BlockSpec((tm, D_MODEL), lambda i: (i, 0)),
            pl.BlockSpec((1, D_MODEL), const),
            pl.BlockSpec((None, D_MODEL, D_FF), lambda i: (layer, 0, 0), pipeline_mode=pl.Buffered(1)),
            pl.BlockSpec((None, D_FF, D_MODEL), lambda i: (layer, 0, 0), pipeline_mode=pl.Buffered(1)),
        ],
        out_specs=pl.BlockSpec((tm, D_MODEL), lambda i: (i, 0)),
        out_shape=jax.ShapeDtypeStruct((tokens, D_MODEL), F32),
        compiler_params=pltpu.CompilerParams(
            dimension_semantics=("arbitrary",), vmem_limit_bytes=VMEM_LIMIT_BYTES),
        name="ffn",
    )(x2d, g_ff_l, w1_all, w2_all)


def _window_bias_table():
    qi = np.arange(A_BLOCK)[:, None]
    si = np.arange(3 * A_BLOCK)[None, :]
    dist = np.abs(si - A_BLOCK - qi).astype(np.float32)
    slopes = (2.0 ** (-8.0 * np.arange(1, A_Q_HEADS + 1) / A_Q_HEADS)).astype(np.float32)
    tab = np.empty((A_KV_HEADS, 2 * A_BLOCK, 6 * A_BLOCK), np.float32)
    for g in range(A_KV_HEADS):
        for hh in range(4):
            rows = slice(A_BLOCK * (hh // 2), A_BLOCK * (hh // 2 + 1))
            cols = slice(3 * A_BLOCK * (hh % 2), 3 * A_BLOCK * (hh % 2 + 1))
            tab[g, rows, cols] = np.where(dist <= WINDOW, -(slopes[4 * g + hh] * dist) * LOG2E, -MASKED)
    return jnp.asarray(tab)


def _score_bound(gain_q, gain_k):
    return 1.02 * HEAD_DIM * (HEAD_DIM ** -0.5 * LOG2E) * jnp.max(jnp.abs(gain_q)) * jnp.max(jnp.abs(gain_k))


def _scalar_table(sink_l, qk_gain_l):
    g = qk_gain_l.astype(F32)
    return jnp.concatenate([sink_l.astype(F32) * LOG2E, _score_bound(g[4], g[5])[None],
                            jnp.zeros((16 - A_Q_HEADS - 1,), F32)])


def _neighbourhood_bias_table(rpb, qk_gain):
    c = np.arange(GRID_W)
    dc = np.clip(c[None, :] - c[:, None] + (NB_COLS - 1), 0, 2 * NB_COLS - 2)
    onehot = jnp.asarray((dc[None] == np.arange(2 * NB_COLS - 1)[:, None, None]).astype(np.float32))
    t = jnp.einsum("lhdc,cqk->lhdqk", rpb.astype(F32), onehot, precision=lax.Precision.HIGHEST) * LOG2E
    cstart = np.clip(c - NB_COLS // 2, 0, GRID_W - NB_COLS)
    cvalid = (c[None, :] >= cstart[:, None]) & (c[None, :] < cstart[:, None] + NB_COLS)
    g = qk_gain.astype(F32)
    qk_bound = jnp.stack([_score_bound(g[l, 2], g[l, 3]) for l in range(DEPTH)])
    bound = qk_bound[:, None] + jnp.max(rpb.astype(F32), axis=(2, 3)) * LOG2E
    t = jnp.where(cvalid, t - bound[:, :, None, None, None], -MASKED)
    return jnp.concatenate([t[:, :, :-1], t[:, :, 1:]], axis=-1)


def _pack_in_proj(w_in):
    qa, ka, va, qb, kb, vb, qm = jnp.split(
        w_in, [A_Q, A_Q + A_KV, A_Q + 2 * A_KV, A_Q + 2 * A_KV + B_W,
               A_Q + 2 * A_KV + 2 * B_W, A_Q + 2 * A_KV + 3 * B_W], axis=-1)
    twice = lambda t: jnp.concatenate(
        [t[..., :HEAD_DIM], t[..., :HEAD_DIM], t[..., HEAD_DIM:], t[..., HEAD_DIM:]], axis=-1)
    return jnp.concatenate([qa, qb, qm, twice(ka), kb, twice(va), vb], axis=-1)


def _pack_qk_gain(qk_gain_l):
    scale = HEAD_DIM ** -0.5 * LOG2E
    g = qk_gain_l.astype(F32)
    return jnp.concatenate([
        jnp.tile(g[0], A_Q_HEADS) * scale, jnp.tile(g[2], B_HEADS) * scale, jnp.tile(g[4], M_HEADS) * scale,
        jnp.tile(g[1], 2 * A_KV_HEADS), jnp.tile(g[3], B_HEADS)])[None, :]


def _trunk(x, kvm, params):
    batch, seq, _ = x.shape
    for l in range(DEPTH):
        p = params[l]
        qkv = _in_proj(x.reshape(batch * seq, D_MODEL), p["g_mix"], params["w_in"], l, p["qk_gain"], params["ones"])
        x = _attention(x, qkv.reshape(batch, seq, PROJ_W),
                       kvm, params["wbias"], params["nat"], l, p["sink"], p["o_gain"], params["w_out"])
        x = _ffn(x.reshape(batch * seq, D_MODEL), p["g_ff"], params["w_ff1"], params["w_ff2"], l)
        x = x.reshape(batch, seq, D_MODEL)
    return x


def kernel(x_prompt, x_sample, mem_prompt, mem_sample, g_mix, w_in, qk_gain, sink, rpb,
           o_gain, w_out, g_mem, w_mem_kv, g_ff, w_ff1, w_ff2):
    ones = jnp.asarray(
        np.kron(np.eye(LANE_TILE // HEAD_DIM), np.full((HEAD_DIM, HEAD_DIM), 1.0 / HEAD_DIM)), BF16)
    params = {
        "ones": ones, "wbias": _window_bias_table(), "nat": _neighbourhood_bias_table(rpb, qk_gain),
        "w_in": _pack_in_proj(w_in.astype(BF16)), "w_out": w_out.astype(BF16),
        "w_ff1": w_ff1.astype(BF16), "w_ff2": w_ff2.astype(BF16),
    }
    for l in range(DEPTH):
        params[l] = {
            "g_mix": g_mix[l].astype(F32)[None, :],
            "qk_gain": _pack_qk_gain(qk_gain[l]),
            "sink": _scalar_table(sink[l], qk_gain[l]),
            "o_gain": o_gain[l].astype(F32)[None, :],
            "g_ff": g_ff[l].astype(F32)[None, :],
        }
    km_gain = jnp.tile(qk_gain[:, 5].astype(F32), (1, M_HEADS))[:, None, :]
    g_mem3 = g_mem.astype(F32)[:, None, :]
    w_mem = w_mem_kv.astype(BF16)
    kvm_prompt = _mem_kv(mem_prompt, g_mem3, w_mem, km_gain, ones)
    kvm_sample = _mem_kv(mem_sample, g_mem3, w_mem, km_gain, ones)
    return (_trunk(x_prompt, kvm_prompt, params), _trunk(x_sample, kvm_sample, params))
```

```python
import functools

import numpy as np
import jax
import jax.numpy as jnp
from jax import lax
from jax.experimental import pallas as pl
from jax.experimental.pallas import tpu as pltpu

D_MODEL = 1024
DEPTH = 2
HEAD_DIM = 64
A_Q_HEADS = 8
A_KV_HEADS = 2
WINDOW = 128
A_BLOCK = 128
B_HEADS = 4
GRID_W = 64
NB_ROWS = 8
NB_COLS = 16
M_HEADS = 4
N_MEM = 256
D_FF = 4 * D_MODEL
EPS = 1e-6

A_Q = A_Q_HEADS * HEAD_DIM
A_KV = A_KV_HEADS * HEAD_DIM
B_W = B_HEADS * HEAD_DIM
M_W = M_HEADS * HEAD_DIM

Q_W = A_Q + B_W + M_W
NORMED_W = Q_W + 2 * 256
PROJ_W = NORMED_W + 2 * 256
LANE_TILE = 256
KV_W = 512
KVA_OFF = Q_W
KVB_OFF = Q_W + KV_W

NAT_HALO = NB_ROWS * GRID_W
MASKED = 1e30
MEM_BOUND_SLOT = A_Q_HEADS
WIN_BOUND_SLOT = A_Q_HEADS + 1
MIN_DENOMINATOR = 2.0 ** -100
LOG2E = float(np.log2(np.e))

PROJ_TILE = 1024
PROJ_SUBTILES = 2
ATTN_TILE = 1024
FFN_TILE = 1024
FF_CHUNK = 1024
MEM_BATCH = 4
VMEM_LIMIT_BYTES = 58 * 1024 * 1024

F32 = jnp.float32
BF16 = jnp.bfloat16


def _dot(a, b):
    return jnp.dot(a, b, preferred_element_type=F32)


def _dot_nt(a, b):
    return lax.dot_general(a, b, (((1,), (1,)), ((), ())), preferred_element_type=F32)


def _rms_scale(x):
    return x * lax.rsqrt(jnp.mean(x * x, axis=-1, keepdims=True) + EPS)


def _head_norm(p, ones, gain):
    ms = _dot((p * p).astype(BF16), ones)
    return p * lax.rsqrt(ms + EPS) * gain


def _lane_mask(width, lo, hi, dtype):
    lane = lax.broadcasted_iota(jnp.int32, (1, width), 1)
    return jnp.where((lane >= lo) & (lane < hi), 1.0, 0.0).astype(dtype)


def _by_head(parts):
    lane = lax.broadcasted_iota(jnp.int32, (1, 4 * HEAD_DIM), 1)
    return jnp.where(lane < 2 * HEAD_DIM,
                     jnp.where(lane < HEAD_DIM, parts[0], parts[1]),
                     jnp.where(lane < 3 * HEAD_DIM, parts[2], parts[3]))


def _mem_kv_kernel(mem_ref, g_ref, w_ref, gain_ref, ones_ref, out_ref):
    nb = mem_ref.shape[0]
    mem = mem_ref[...].reshape(nb * N_MEM, D_MODEL)
    h = (_rms_scale(mem) * g_ref[...]).astype(BF16)
    mkv = _dot(h, w_ref[...])
    km = _head_norm(mkv[:, 0:M_W], ones_ref[...], gain_ref[...])
    out_ref[:, :, 0:M_W] = km.astype(BF16).reshape(nb, N_MEM, M_W)
    out_ref[:, :, M_W:2 * M_W] = mkv[:, M_W:2 * M_W].astype(BF16).reshape(nb, N_MEM, M_W)


def _mem_kv(mem, g_mem, w_mem_kv, km_gain, ones):
    batch = mem.shape[0]
    nb = MEM_BATCH
    assert batch % nb == 0
    return pl.pallas_call(
        _mem_kv_kernel,
        grid=(DEPTH, batch // nb),
        in_specs=[
            pl.BlockSpec((nb, N_MEM, D_MODEL), lambda l, b: (b, 0, 0)),
            pl.BlockSpec((None, 1, D_MODEL), lambda l, b: (l, 0, 0)),
            pl.BlockSpec((None, D_MODEL, 2 * M_W), lambda l, b: (l, 0, 0)),
            pl.BlockSpec((None, 1, M_W), lambda l, b: (l, 0, 0)),
            pl.BlockSpec((LANE_TILE, LANE_TILE), lambda l, b: (0, 0)),
        ],
        out_specs=pl.BlockSpec((None, nb, N_MEM, 2 * M_W), lambda l, b: (l, b, 0, 0)),
        out_shape=jax.ShapeDtypeStruct((DEPTH, batch, N_MEM, 2 * M_W), BF16),
        compiler_params=pltpu.CompilerParams(
            dimension_semantics=("arbitrary", "arbitrary"), vmem_limit_bytes=VMEM_LIMIT_BYTES),
        name="mem_kv",
    )(mem, g_mem, w_mem_kv, km_gain, ones)


def _in_proj_kernel(x_ref, g_ref, w_ref, gain_ref, ones_ref, qkv_ref):
    ones = ones_ref[...]
    offs = (0, 256, 512, 768, KVA_OFF, KVB_OFF, KVA_OFF + 256, KVB_OFF + 256)
    sub = x_ref.shape[0] // PROJ_SUBTILES
    for t in range(PROJ_SUBTILES):
        rows = slice(t * sub, (t + 1) * sub)
        h = (_rms_scale(x_ref[rows, :]) * g_ref[...]).astype(BF16)
        for pair in range(PROJ_W // (2 * LANE_TILE)):
            proj = _dot(h, w_ref[:, 2 * pair * LANE_TILE:2 * (pair + 1) * LANE_TILE])
            for half in range(2):
                c = 2 * pair + half
                chunk = proj[:, half * LANE_TILE:(half + 1) * LANE_TILE]
                if c * LANE_TILE < NORMED_W:
                    chunk = _head_norm(chunk, ones, gain_ref[:, c * LANE_TILE:(c + 1) * LANE_TILE])
                qkv_ref[rows, offs[c]:offs[c] + LANE_TILE] = chunk.astype(BF16)


def _in_proj(x2d, g_mix_l, w_in_all, layer, gain_l, ones):
    tokens = x2d.shape[0]
    tm = PROJ_TILE
    assert tokens % tm == 0
    const = lambda i: (0, 0)
    return pl.pallas_call(
        _in_proj_kernel,
        grid=(tokens // tm,),
        in_specs=[
            pl.BlockSpec((tm, D_MODEL), lambda i: (i, 0)),
            pl.BlockSpec((1, D_MODEL), const),
            pl.BlockSpec((None, D_MODEL, PROJ_W), lambda i: (layer, 0, 0)),
            pl.BlockSpec((1, NORMED_W), const),
            pl.BlockSpec((LANE_TILE, LANE_TILE), const),
        ],
        out_specs=pl.BlockSpec((tm, PROJ_W), lambda i: (i, 0)),
        out_shape=jax.ShapeDtypeStruct((tokens, PROJ_W), BF16),
        compiler_params=pltpu.CompilerParams(
            dimension_semantics=("arbitrary",), vmem_limit_bytes=VMEM_LIMIT_BYTES),
        name="in_proj",
    )(x2d, g_mix_l, w_in_all, gain_l, ones)


def _attn_kernel(seq, tq,
                 x_ref, q_ref, prev_ref, next_ref, kvm_ref,
                 wbias_ref, nat_ref, sink_ref, og_ref, wout_ref, out_ref,
                 kw_buf, vw_buf, kb_buf):
    i = pl.program_id(1)
    n_blocks = seq // A_BLOCK
    n_rows = seq // GRID_W
    blocks_per_tile = tq // A_BLOCK
    rows_per_tile = tq // GRID_W
    o_buf = out_ref

    lo_b = _lane_mask(128, 0, HEAD_DIM, BF16)
    hi_b = _lane_mask(128, HEAD_DIM, 128, BF16)
    pieces = ((prev_ref, slice(NAT_HALO - A_BLOCK, NAT_HALO), 0, 0, A_BLOCK),
              (q_ref, slice(0, tq), KVA_OFF, A_BLOCK, tq),
              (next_ref, slice(0, A_BLOCK), 0, A_BLOCK + tq, A_BLOCK))
    for src, src_rows, lane0, r0, n in pieces:
        rows = slice(r0, r0 + n)
        for g in range(A_KV_HEADS):
            k2 = src[src_rows, lane0 + 128 * g:lane0 + 128 * (g + 1)]
            v2 = src[src_rows, lane0 + 256 + 128 * g:lane0 + 256 + 128 * (g + 1)]
            kw_buf[rows, 256 * g:256 * g + 128] = k2 * lo_b
            kw_buf[rows, 256 * g + 128:256 * (g + 1)] = k2 * hi_b
            vw_buf[rows, 512 * g:512 * g + 128] = v2 * lo_b
            vw_buf[rows, 512 * g + 128:512 * g + 256] = jnp.broadcast_to(_lane_mask(128, 0, 1, BF16), (n, 128))
            vw_buf[rows, 512 * g + 256:512 * g + 384] = v2 * hi_b
            vw_buf[rows, 512 * g + 384:512 * (g + 1)] = jnp.broadcast_to(_lane_mask(128, 1, 2, BF16), (n, 128))
    kb_buf[0:NAT_HALO] = prev_ref[:, KV_W:2 * KV_W]
    kb_buf[NAT_HALO:NAT_HALO + tq] = q_ref[:, KVB_OFF:KVB_OFF + KV_W]
    kb_buf[NAT_HALO + tq:2 * NAT_HALO + tq] = next_ref[:, KV_W:2 * KV_W]

    top_rows = lax.broadcasted_iota(jnp.int32, (2 * A_BLOCK, 1), 0) < A_BLOCK
    low_lanes = lax.broadcasted_iota(jnp.int32, (1, 128), 1) < HEAD_DIM

    band_col = lax.broadcasted_iota(jnp.int32, (1, 6 * A_BLOCK), 1)
    band_col = jnp.where(band_col >= 3 * A_BLOCK, band_col - 3 * A_BLOCK, band_col)
    no_leading_keys = jnp.where(band_col < A_BLOCK, -MASKED, 0.0)
    no_trailing_keys = jnp.where(band_col >= 2 * A_BLOCK, -MASKED, 0.0)

    def window_block(bounded, j, l_min):
        blk = i * blocks_per_tile + j
        edge = jnp.where(blk == 0, no_leading_keys, jnp.where(blk == n_blocks - 1, no_trailing_keys, 0.0))
        q0 = pl.multiple_of(j * A_BLOCK, A_BLOCK)
        band = pl.ds(q0, 3 * A_BLOCK)
        for g in range(A_KV_HEADS):
            qg = q_ref[pl.ds(q0, A_BLOCK), 256 * g:256 * (g + 1)]
            lhs = jnp.concatenate([qg[:, 0:128], qg[:, 128:256]], axis=0)
            keys = jnp.concatenate(
                [kw_buf[band, 256 * g:256 * g + 128], kw_buf[band, 256 * g + 128:256 * (g + 1)]], axis=0)
            s = _dot_nt(lhs, keys) + wbias_ref[g] + edge
            sink_lo = jnp.where(top_rows, sink_ref[4 * g], sink_ref[4 * g + 2])
            sink_hi = jnp.where(top_rows, sink_ref[4 * g + 1], sink_ref[4 * g + 3])
            s_lo = s[:, 0:3 * A_BLOCK]
            s_hi = s[:, 3 * A_BLOCK:6 * A_BLOCK]
            m_lo = sink_ref[WIN_BOUND_SLOT] if bounded else jnp.max(s_lo, axis=-1, keepdims=True)
            m_hi = sink_ref[WIN_BOUND_SLOT] if bounded else jnp.max(s_hi, axis=-1, keepdims=True)
            p = jnp.concatenate([jnp.exp2(s_lo - m_lo), jnp.exp2(s_hi - m_hi)], axis=1).astype(BF16)
            vals = jnp.concatenate(
                [vw_buf[band, 512 * g:512 * g + 256], vw_buf[band, 512 * g + 256:512 * (g + 1)]], axis=0)
            o = _dot(p, vals)
            l_lo = o[:, 128:129] + jnp.exp2(sink_lo - m_lo)
            l_hi = o[:, 129:130] + jnp.exp2(sink_hi - m_hi)
            res = o[:, 0:128] / jnp.where(low_lanes, l_lo, l_hi)
            o_buf[pl.ds(q0, A_BLOCK), 256 * g:256 * g + 128] = res[0:A_BLOCK]
            o_buf[pl.ds(q0, A_BLOCK), 256 * g + 128:256 * (g + 1)] = res[A_BLOCK:2 * A_BLOCK]
            l_min = jnp.minimum(l_min, jnp.minimum(o[:, 128:129], o[:, 129:130]))
        return l_min

    def windowed(bounded):
        l_min = lax.fori_loop(0, blocks_per_tile, functools.partial(window_block, bounded),
                              jnp.full((2 * A_BLOCK, 1), jnp.inf, F32), unroll=blocks_per_tile)
        return jnp.min(l_min)

    head_b = [_lane_mask(B_W, HEAD_DIM * h, HEAD_DIM * (h + 1), BF16) for h in range(B_HEADS)]

    def nat_scores(rr):
        r = i * rows_per_tile + rr
        start = jnp.clip(r - NB_ROWS // 2, 0, n_rows - NB_ROWS)
        d0 = (NB_ROWS - 1) - (r - start)
        k0 = pl.multiple_of((start - i * rows_per_tile + NB_ROWS) * GRID_W, GRID_W)
        kwin = kb_buf[pl.ds(k0, NAT_HALO), 0:B_W]
        qrow = q_ref[rr * GRID_W:(rr + 1) * GRID_W, A_Q:A_Q + B_W]
        lhs = jnp.concatenate([qrow * head_b[h] for h in range(B_HEADS)], axis=0)
        bias = jnp.concatenate(
            [jnp.concatenate([nat_ref[h, d0 + 2 * m] for m in range(NB_ROWS // 2)], axis=1)
             for h in range(B_HEADS)], axis=0)
        return _dot_nt(lhs, kwin) + bias, k0

    def nat_finish(rr, s, k0, bounded):
        vwin = kb_buf[pl.ds(k0, NAT_HALO), B_W:2 * B_W]
        p = jnp.exp2(s) if bounded else jnp.exp2(s - jnp.max(s, axis=-1, keepdims=True))
        l = jnp.sum(p, axis=-1, keepdims=True)
        pv = _dot(p.astype(BF16), vwin)
        rows = [slice(GRID_W * h, GRID_W * (h + 1)) for h in range(B_HEADS)]
        o_buf[rr * GRID_W:(rr + 1) * GRID_W, A_Q:A_Q + B_W] = (
            _by_head([pv[r] for r in rows]) / _by_head([l[r] for r in rows]))
        return l

    def neighbourhood_and_memory(bounded):
        pending = nat_scores(0)
        l_min = None
        for rr in range(1, rows_per_tile + 1):
            ahead = nat_scores(rr) if rr < rows_per_tile else None
            l = nat_finish(rr - 1, *pending, bounded)
            l_min = l if l_min is None else jnp.minimum(l_min, l)
            pending = ahead

        km = kvm_ref[:, 0:M_W]
        vm = kvm_ref[:, M_W:2 * M_W]
        qm = q_ref[:, A_Q + B_W:Q_W]
        s = _dot_nt(qm, jnp.concatenate([km * head_b[h] for h in range(M_HEADS)], axis=0))
        probs, sums = [], []
        for h in range(M_HEADS):
            s_h = s[:, N_MEM * h:N_MEM * (h + 1)]
            shift = sink_ref[MEM_BOUND_SLOT] if bounded else jnp.max(s_h, axis=-1, keepdims=True)
            p = jnp.exp2(s_h - shift)
            sums.append(jnp.sum(p, axis=-1, keepdims=True))
            probs.append(p.astype(BF16))
        om = _dot(jnp.concatenate(probs, axis=1),
                  jnp.concatenate([vm * head_b[h] for h in range(M_HEADS)], axis=0))
        o_buf[:, A_Q + B_W:Q_W] = om / _by_head(sums)
        m_min = jnp.minimum(jnp.minimum(sums[0], sums[1]), jnp.minimum(sums[2], sums[3]))
        return jnp.minimum(jnp.min(l_min), jnp.min(m_min))

    smallest = jnp.minimum(windowed(True), neighbourhood_and_memory(True))

    @pl.when(smallest < MIN_DENOMINATOR)
    def _():
        windowed(False)
        neighbourhood_and_memory(False)

    o = jnp.concatenate(
        [_rms_scale(o_buf[:, 0:A_Q]), _rms_scale(o_buf[:, A_Q:A_Q + B_W]), _rms_scale(o_buf[:, A_Q + B_W:Q_W])],
        axis=-1) * og_ref[...]
    out_ref[...] = x_ref[...] + _dot(o.astype(BF16), wout_ref[...])


def _attention(x, qkv, kvm_all, wbias_tab, nat_all, layer, sink_l, og_l, wout_all):
    batch, seq, _ = x.shape
    tq = ATTN_TILE
    assert seq % tq == 0 and tq % NAT_HALO == 0 and seq >= 2 * A_BLOCK and seq >= NB_ROWS * GRID_W
    nt = seq // tq
    halo_per_tile = tq // NAT_HALO
    n_halo_blocks = seq // NAT_HALO
    const2 = lambda b, i: (0, 0)
    once = pl.Buffered(1)
    return pl.pallas_call(
        functools.partial(_attn_kernel, seq, tq),
        grid=(batch, nt),
        in_specs=[
            pl.BlockSpec((None, tq, D_MODEL), lambda b, i: (b, i, 0)),
            pl.BlockSpec((None, tq, PROJ_W), lambda b, i: (b, i, 0)),
            pl.BlockSpec((None, NAT_HALO, 2 * KV_W),
                         lambda b, i: (b, jnp.maximum(i * halo_per_tile - 1, 0), 1)),
            pl.BlockSpec((None, NAT_HALO, 2 * KV_W),
                         lambda b, i: (b, jnp.minimum((i + 1) * halo_per_tile, n_halo_blocks - 1), 1)),
            pl.BlockSpec((None, None, N_MEM, 2 * M_W), lambda b, i: (layer, b, 0, 0)),
            pl.BlockSpec((A_KV_HEADS, 2 * A_BLOCK, 6 * A_BLOCK), lambda b, i: (0, 0, 0), pipeline_mode=once),
            pl.BlockSpec((None, B_HEADS, 2 * NB_ROWS - 2, GRID_W, 2 * GRID_W),
                         lambda b, i: (layer, 0, 0, 0, 0), pipeline_mode=once),
            pl.BlockSpec(memory_space=pltpu.SMEM),
            pl.BlockSpec((1, Q_W), const2),
            pl.BlockSpec((None, Q_W, D_MODEL), lambda b, i: (layer, 0, 0), pipeline_mode=once),
        ],
        out_specs=pl.BlockSpec((None, tq, D_MODEL), lambda b, i: (b, i, 0)),
        out_shape=jax.ShapeDtypeStruct((batch, seq, D_MODEL), F32),
        scratch_shapes=[
            pltpu.VMEM((tq + 2 * A_BLOCK, 2 * 256), BF16),
            pltpu.VMEM((tq + 2 * A_BLOCK, 2 * 512), BF16),
            pltpu.VMEM((tq + 2 * NAT_HALO, KV_W), BF16),
        ],
        compiler_params=pltpu.CompilerParams(
            dimension_semantics=("arbitrary", "arbitrary"), vmem_limit_bytes=VMEM_LIMIT_BYTES),
        name="attention",
    )(x, qkv, qkv, qkv, kvm_all, wbias_tab, nat_all, sink_l, og_l, wout_all)


def _ffn_kernel(x_ref, g_ref, w1_ref, w2_ref, out_ref):
    x = x_ref[...]
    h = (_rms_scale(x) * g_ref[...]).astype(BF16)
    acc = x
    for c in range(D_FF // FF_CHUNK):
        f = _dot(h, w1_ref[:, c * FF_CHUNK:(c + 1) * FF_CHUNK])
        a = jnp.square(jnp.maximum(f, 0.0)).astype(BF16)
        acc = acc + _dot(a, w2_ref[c * FF_CHUNK:(c + 1) * FF_CHUNK, :])
    out_ref[...] = acc


def _ffn(x2d, g_ff_l, w1_all, w2_all, layer):
    tokens = x2d.shape[0]
    tm = FFN_TILE
    assert tokens % tm == 0
    const = lambda i: (0, 0)
    return pl.pallas_call(
        _ffn_kernel,
        grid=(tokens // tm,),
        in_specs=[
            pl.BlockSpec((tm, D_MODEL), lambda i: (i, 0)),
            pl.BlockSpec((1, D_MODEL), const),
            pl.BlockSpec((None, D_MODEL, D_FF), lambda i: (layer, 0, 0), pipeline_mode=pl.Buffered(1)),
            pl.BlockSpec((None, D_FF, D_MODEL), lambda i: (layer, 0, 0), pipeline_mode=pl.Buffered(1)),
        ],
        out_specs=pl.BlockSpec((tm, D_MODEL), lambda i: (i, 0)),
        out_shape=jax.ShapeDtypeStruct((tokens, D_MODEL), F32),
        compiler_params=pltpu.CompilerParams(
            dimension_semantics=("arbitrary",), vmem_limit_bytes=VMEM_LIMIT_BYTES),
        name="ffn",
    )(x2d, g_ff_l, w1_all, w2_all)


def _window_bias_table():
    qi = np.arange(A_BLOCK)[:, None]
    si = np.arange(3 * A_BLOCK)[None, :]
    dist = np.abs(si - A_BLOCK - qi).astype(np.float32)
    slopes = (2.0 ** (-8.0 * np.arange(1, A_Q_HEADS + 1) / A_Q_HEADS)).astype(np.float32)
    tab = np.empty((A_KV_HEADS, 2 * A_BLOCK, 6 * A_BLOCK), np.float32)
    for g in range(A_KV_HEADS):
        for hh in range(4):
            rows = slice(A_BLOCK * (hh // 2), A_BLOCK * (hh // 2 + 1))
            cols = slice(3 * A_BLOCK * (hh % 2), 3 * A_BLOCK * (hh % 2 + 1))
            tab[g, rows, cols] = np.where(dist <= WINDOW, -(slopes[4 * g + hh] * dist) * LOG2E, -MASKED)
    return jnp.asarray(tab)


def _score_bound(gain_q, gain_k):
    return 1.02 * HEAD_DIM * (HEAD_DIM ** -0.5 * LOG2E) * jnp.max(jnp.abs(gain_q)) * jnp.max(jnp.abs(gain_k))


def _scalar_table(sink_l, qk_gain_l):
    g = qk_gain_l.astype(F32)
    return jnp.concatenate([sink_l.astype(F32) * LOG2E, _score_bound(g[4], g[5])[None], _score_bound(g[0], g[1])[None],
                            jnp.zeros((16 - A_Q_HEADS - 2,), F32)])


def _neighbourhood_bias_table(rpb, qk_gain):
    c = np.arange(GRID_W)
    dc = np.clip(c[None, :] - c[:, None] + (NB_COLS - 1), 0, 2 * NB_COLS - 2)
    onehot = jnp.asarray((dc[None] == np.arange(2 * NB_COLS - 1)[:, None, None]).astype(np.float32))
    t = jnp.einsum("lhdc,cqk->lhdqk", rpb.astype(F32), onehot, precision=lax.Precision.HIGHEST) * LOG2E
    cstart = np.clip(c - NB_COLS // 2, 0, GRID_W - NB_COLS)
    cvalid = (c[None, :] >= cstart[:, None]) & (c[None, :] < cstart[:, None] + NB_COLS)
    g = qk_gain.astype(F32)
    qk_bound = jnp.stack([_score_bound(g[l, 2], g[l, 3]) for l in range(DEPTH)])
    bound = qk_bound[:, None] + jnp.max(rpb.astype(F32), axis=(2, 3)) * LOG2E
    t = jnp.where(cvalid, t - bound[:, :, None, None, None], -MASKED)
    return jnp.concatenate([t[:, :, :-1], t[:, :, 1:]], axis=-1)


def _pack_in_proj(w_in):
    qa, ka, va, qb, kb, vb, qm = jnp.split(
        w_in, [A_Q, A_Q + A_KV, A_Q + 2 * A_KV, A_Q + 2 * A_KV + B_W,
               A_Q + 2 * A_KV + 2 * B_W, A_Q + 2 * A_KV + 3 * B_W], axis=-1)
    twice = lambda t: jnp.concatenate(
        [t[..., :HEAD_DIM], t[..., :HEAD_DIM], t[..., HEAD_DIM:], t[..., HEAD_DIM:]], axis=-1)
    return jnp.concatenate([qa, qb, qm, twice(ka), kb, twice(va), vb], axis=-1).astype(BF16)


def _pack_qk_gain(qk_gain_l):
    scale = HEAD_DIM ** -0.5 * LOG2E
    g = qk_gain_l.astype(F32)
    return jnp.concatenate([
        jnp.tile(g[0], A_Q_HEADS) * scale, jnp.tile(g[2], B_HEADS) * scale, jnp.tile(g[4], M_HEADS) * scale,
        jnp.tile(g[1], 2 * A_KV_HEADS), jnp.tile(g[3], B_HEADS)])[None, :]


def _trunk(x, kvm, params):
    batch, seq, _ = x.shape
    for l in range(DEPTH):
        p = params[l]
        qkv = _in_proj(x.reshape(batch * seq, D_MODEL), p["g_mix"], params["w_in"], l, p["qk_gain"], params["ones"])
        x = _attention(x, qkv.reshape(batch, seq, PROJ_W),
                       kvm, params["wbias"], params["nat"], l, p["sink"], p["o_gain"], params["w_out"])
        x = _ffn(x.reshape(batch * seq, D_MODEL), p["g_ff"], params["w_ff1"], params["w_ff2"], l)
        x = x.reshape(batch, seq, D_MODEL)
    return x


def kernel(x_prompt, x_sample, mem_prompt, mem_sample, g_mix, w_in, qk_gain, sink, rpb,
           o_gain, w_out, g_mem, w_mem_kv, g_ff, w_ff1, w_ff2):
    ones = jnp.asarray(
        np.kron(np.eye(LANE_TILE // HEAD_DIM), np.full((HEAD_DIM, HEAD_DIM), 1.0 / HEAD_DIM)), BF16)
    params = {
        "ones": ones, "wbias": _window_bias_table(), "nat": _neighbourhood_bias_table(rpb, qk_gain),
        "w_in": _pack_in_proj(w_in), "w_out": w_out.astype(BF16),
        "w_ff1": w_ff1.astype(BF16), "w_ff2": w_ff2.astype(BF16),
    }
    for l in range(DEPTH):
        params[l] = {
            "g_mix": g_mix[l].astype(F32)[None, :],
            "qk_gain": _pack_qk_gain(qk_gain[l]),
            "sink": _scalar_table(sink[l], qk_gain[l]),
            "o_gain": o_gain[l].astype(F32)[None, :],
            "g_ff": g_ff[l].astype(F32)[None, :],
        }
    km_gain = jnp.tile(qk_gain[:, 5].astype(F32), (1, M_HEADS))[:, None, :]
    g_mem3 = g_mem.astype(F32)[:, None, :]
    w_mem = w_mem_kv.astype(BF16)
    kvm_prompt = _mem_kv(mem_prompt, g_mem3, w_mem, km_gain, ones)
    kvm_sample = _mem_kv(mem_sample, g_mem3, w_mem, km_gain, ones)
    return (_trunk(x_prompt, kvm_prompt, params), _trunk(x_sample, kvm_sample, params))
```

```python
import functools

import numpy as np
import jax
import jax.numpy as jnp
from jax import lax
from jax.experimental import pallas as pl
from jax.experimental.pallas import tpu as pltpu

D_MODEL = 1024
DEPTH = 2
HEAD_DIM = 64
A_Q_HEADS = 8
A_KV_HEADS = 2
WINDOW = 128
A_BLOCK = 128
B_HEADS = 4
GRID_W = 64
NB_ROWS = 8
NB_COLS = 16
M_HEADS = 4
N_MEM = 256
D_FF = 4 * D_MODEL
EPS = 1e-6

A_Q = A_Q_HEADS * HEAD_DIM
A_KV = A_KV_HEADS * HEAD_DIM
B_W = B_HEADS * HEAD_DIM
M_W = M_HEADS * HEAD_DIM

Q_W = A_Q + B_W + M_W
NORMED_W = Q_W + 2 * 256
PROJ_W = NORMED_W + 2 * 256
LANE_TILE = 256
KV_W = 512
KVA_OFF = Q_W
KVB_OFF = Q_W + KV_W

NAT_HALO = NB_ROWS * GRID_W
MASKED = 1e30
MEM_BOUND_SLOT = A_Q_HEADS
MIN_DENOMINATOR = 2.0 ** -100
LOG2E = float(np.log2(np.e))

PROJ_TILE = 1024
PROJ_SUBTILES = 2
ATTN_TILE = 1024
FFN_TILE = 1024
FF_CHUNK = 1024
MEM_BATCH = 4
VMEM_LIMIT_BYTES = 58 * 1024 * 1024

F32 = jnp.float32
BF16 = jnp.bfloat16


def _dot(a, b):
    return jnp.dot(a, b, preferred_element_type=F32)


def _dot_nt(a, b):
    return lax.dot_general(a, b, (((1,), (1,)), ((), ())), preferred_element_type=F32)


def _rms_scale(x):
    return x * lax.rsqrt(jnp.mean(x * x, axis=-1, keepdims=True) + EPS)


def _head_norm(p, ones, gain):
    ms = _dot((p * p).astype(BF16), ones)
    return p * lax.rsqrt(ms + EPS) * gain


def _lane_mask(width, lo, hi, dtype):
    lane = lax.broadcasted_iota(jnp.int32, (1, width), 1)
    return jnp.where((lane >= lo) & (lane < hi), 1.0, 0.0).astype(dtype)


def _by_head(parts):
    lane = lax.broadcasted_iota(jnp.int32, (1, 4 * HEAD_DIM), 1)
    return jnp.where(lane < 2 * HEAD_DIM,
                     jnp.where(lane < HEAD_DIM, parts[0], parts[1]),
                     jnp.where(lane < 3 * HEAD_DIM, parts[2], parts[3]))


def _mem_kv_kernel(mem_ref, g_ref, w_ref, gain_ref, ones_ref, out_ref):
    nb = mem_ref.shape[0]
    mem = mem_ref[...].reshape(nb * N_MEM, D_MODEL)
    h = (_rms_scale(mem) * g_ref[...]).astype(BF16)
    mkv = _dot(h, w_ref[...])
    km = _head_norm(mkv[:, 0:M_W], ones_ref[...], gain_ref[...])
    out_ref[:, :, 0:M_W] = km.astype(BF16).reshape(nb, N_MEM, M_W)
    out_ref[:, :, M_W:2 * M_W] = mkv[:, M_W:2 * M_W].astype(BF16).reshape(nb, N_MEM, M_W)


def _mem_kv(mem, g_mem, w_mem_kv, km_gain, ones):
    batch = mem.shape[0]
    nb = MEM_BATCH
    assert batch % nb == 0
    return pl.pallas_call(
        _mem_kv_kernel,
        grid=(DEPTH, batch // nb),
        in_specs=[
            pl.BlockSpec((nb, N_MEM, D_MODEL), lambda l, b: (b, 0, 0)),
            pl.BlockSpec((None, 1, D_MODEL), lambda l, b: (l, 0, 0)),
            pl.BlockSpec((None, D_MODEL, 2 * M_W), lambda l, b: (l, 0, 0)),
            pl.BlockSpec((None, 1, M_W), lambda l, b: (l, 0, 0)),
            pl.BlockSpec((LANE_TILE, LANE_TILE), lambda l, b: (0, 0)),
        ],
        out_specs=pl.BlockSpec((None, nb, N_MEM, 2 * M_W), lambda l, b: (l, b, 0, 0)),
        out_shape=jax.ShapeDtypeStruct((DEPTH, batch, N_MEM, 2 * M_W), BF16),
        compiler_params=pltpu.CompilerParams(
            dimension_semantics=("arbitrary", "arbitrary"), vmem_limit_bytes=VMEM_LIMIT_BYTES),
        name="mem_kv",
    )(mem, g_mem, w_mem_kv, km_gain, ones)


def _in_proj_kernel(x_ref, g_ref, w_ref, gain_ref, ones_ref, qkv_ref):
    ones = ones_ref[...]
    offs = (0, 256, 512, 768, KVA_OFF, KVB_OFF, KVA_OFF + 256, KVB_OFF + 256)
    sub = x_ref.shape[0] // PROJ_SUBTILES
    for t in range(PROJ_SUBTILES):
        rows = slice(t * sub, (t + 1) * sub)
        h = (_rms_scale(x_ref[rows, :]) * g_ref[...]).astype(BF16)
        for pair in range(PROJ_W // (2 * LANE_TILE)):
            proj = _dot(h, w_ref[:, 2 * pair * LANE_TILE:2 * (pair + 1) * LANE_TILE])
            for half in range(2):
                c = 2 * pair + half
                chunk = proj[:, half * LANE_TILE:(half + 1) * LANE_TILE]
                if c * LANE_TILE < NORMED_W:
                    chunk = _head_norm(chunk, ones, gain_ref[:, c * LANE_TILE:(c + 1) * LANE_TILE])
                qkv_ref[rows, offs[c]:offs[c] + LANE_TILE] = chunk.astype(BF16)


def _in_proj(x2d, g_mix_l, w_in_all, layer, gain_l, ones):
    tokens = x2d.shape[0]
    tm = PROJ_TILE
    assert tokens % tm == 0
    const = lambda i: (0, 0)
    return pl.pallas_call(
        _in_proj_kernel,
        grid=(tokens // tm,),
        in_specs=[
            pl.BlockSpec((tm, D_MODEL), lambda i: (i, 0)),
            pl.BlockSpec((1, D_MODEL), const),
            pl.BlockSpec((None, D_MODEL, PROJ_W), lambda i: (layer, 0, 0)),
            pl.BlockSpec((1, NORMED_W), const),
            pl.BlockSpec((LANE_TILE, LANE_TILE), const),
        ],
        out_specs=pl.BlockSpec((tm, PROJ_W), lambda i: (i, 0)),
        out_shape=jax.ShapeDtypeStruct((tokens, PROJ_W), BF16),
        compiler_params=pltpu.CompilerParams(
            dimension_semantics=("arbitrary",), vmem_limit_bytes=VMEM_LIMIT_BYTES),
        name="in_proj",
    )(x2d, g_mix_l, w_in_all, gain_l, ones)


def _attn_kernel(seq, tq,
                 x_ref, q_ref, prev_ref, next_ref, kvm_ref,
                 wbias_ref, nat_ref, sink_ref, og_ref, wout_ref, out_ref,
                 kw_buf, vw_buf, kb_buf):
    i = pl.program_id(1)
    n_blocks = seq // A_BLOCK
    n_rows = seq // GRID_W
    blocks_per_tile = tq // A_BLOCK
    rows_per_tile = tq // GRID_W
    o_buf = out_ref

    lo_b = _lane_mask(128, 0, HEAD_DIM, BF16)
    hi_b = _lane_mask(128, HEAD_DIM, 128, BF16)
    pieces = ((prev_ref, slice(NAT_HALO - A_BLOCK, NAT_HALO), 0, 0, A_BLOCK),
              (q_ref, slice(0, tq), KVA_OFF, A_BLOCK, tq),
              (next_ref, slice(0, A_BLOCK), 0, A_BLOCK + tq, A_BLOCK))
    for src, src_rows, lane0, r0, n in pieces:
        rows = slice(r0, r0 + n)
        for g in range(A_KV_HEADS):
            k2 = src[src_rows, lane0 + 128 * g:lane0 + 128 * (g + 1)]
            v2 = src[src_rows, lane0 + 256 + 128 * g:lane0 + 256 + 128 * (g + 1)]
            kw_buf[rows, 256 * g:256 * g + 128] = k2 * lo_b
            kw_buf[rows, 256 * g + 128:256 * (g + 1)] = k2 * hi_b
            vw_buf[rows, 512 * g:512 * g + 128] = v2 * lo_b
            vw_buf[rows, 512 * g + 128:512 * g + 256] = jnp.broadcast_to(_lane_mask(128, 0, 1, BF16), (n, 128))
            vw_buf[rows, 512 * g + 256:512 * g + 384] = v2 * hi_b
            vw_buf[rows, 512 * g + 384:512 * (g + 1)] = jnp.broadcast_to(_lane_mask(128, 1, 2, BF16), (n, 128))
    kb_buf[0:NAT_HALO] = prev_ref[:, KV_W:2 * KV_W]
    kb_buf[NAT_HALO:NAT_HALO + tq] = q_ref[:, KVB_OFF:KVB_OFF + KV_W]
    kb_buf[NAT_HALO + tq:2 * NAT_HALO + tq] = next_ref[:, KV_W:2 * KV_W]

    top_rows = lax.broadcasted_iota(jnp.int32, (2 * A_BLOCK, 1), 0) < A_BLOCK
    low_lanes = lax.broadcasted_iota(jnp.int32, (1, 128), 1) < HEAD_DIM

    band_col = lax.broadcasted_iota(jnp.int32, (1, 6 * A_BLOCK), 1)
    band_col = jnp.where(band_col >= 3 * A_BLOCK, band_col - 3 * A_BLOCK, band_col)
    no_leading_keys = jnp.where(band_col < A_BLOCK, -MASKED, 0.0)
    no_trailing_keys = jnp.where(band_col >= 2 * A_BLOCK, -MASKED, 0.0)

    def window_block(j, carry):
        blk = i * blocks_per_tile + j
        edge = jnp.where(blk == 0, no_leading_keys, jnp.where(blk == n_blocks - 1, no_trailing_keys, 0.0))
        q0 = pl.multiple_of(j * A_BLOCK, A_BLOCK)
        band = pl.ds(q0, 3 * A_BLOCK)
        for g in range(A_KV_HEADS):
            qg = q_ref[pl.ds(q0, A_BLOCK), 256 * g:256 * (g + 1)]
            lhs = jnp.concatenate([qg[:, 0:128], qg[:, 128:256]], axis=0)
            keys = jnp.concatenate(
                [kw_buf[band, 256 * g:256 * g + 128], kw_buf[band, 256 * g + 128:256 * (g + 1)]], axis=0)
            s = _dot_nt(lhs, keys) + wbias_ref[g] + edge
            sink_lo = jnp.where(top_rows, sink_ref[4 * g], sink_ref[4 * g + 2])
            sink_hi = jnp.where(top_rows, sink_ref[4 * g + 1], sink_ref[4 * g + 3])
            s_lo = s[:, 0:3 * A_BLOCK]
            s_hi = s[:, 3 * A_BLOCK:6 * A_BLOCK]
            m_lo = jnp.max(s_lo, axis=-1, keepdims=True)
            m_hi = jnp.max(s_hi, axis=-1, keepdims=True)
            p = jnp.concatenate([jnp.exp2(s_lo - m_lo), jnp.exp2(s_hi - m_hi)], axis=1).astype(BF16)
            vals = jnp.concatenate(
                [vw_buf[band, 512 * g:512 * g + 256], vw_buf[band, 512 * g + 256:512 * (g + 1)]], axis=0)
            o = _dot(p, vals)
            l_lo = o[:, 128:129] + jnp.exp2(sink_lo - m_lo)
            l_hi = o[:, 129:130] + jnp.exp2(sink_hi - m_hi)
            res = o[:, 0:128] / jnp.where(low_lanes, l_lo, l_hi)
            o_buf[pl.ds(q0, A_BLOCK), 256 * g:256 * g + 128] = res[0:A_BLOCK]
            o_buf[pl.ds(q0, A_BLOCK), 256 * g + 128:256 * (g + 1)] = res[A_BLOCK:2 * A_BLOCK]
        return carry

    lax.fori_loop(0, blocks_per_tile, window_block, 0, unroll=blocks_per_tile)

    head_b = [_lane_mask(B_W, HEAD_DIM * h, HEAD_DIM * (h + 1), BF16) for h in range(B_HEADS)]

    def nat_scores(rr):
        r = i * rows_per_tile + rr
        start = jnp.clip(r - NB_ROWS // 2, 0, n_rows - NB_ROWS)
        d0 = (NB_ROWS - 1) - (r - start)
        k0 = pl.multiple_of((start - i * rows_per_tile + NB_ROWS) * GRID_W, GRID_W)
        kwin = kb_buf[pl.ds(k0, NAT_HALO), 0:B_W]
        qrow = q_ref[rr * GRID_W:(rr + 1) * GRID_W, A_Q:A_Q + B_W]
        lhs = jnp.concatenate([qrow * head_b[h] for h in range(B_HEADS)], axis=0)
        bias = jnp.concatenate(
            [jnp.concatenate([nat_ref[h, d0 + 2 * m] for m in range(NB_ROWS // 2)], axis=1)
             for h in range(B_HEADS)], axis=0)
        return _dot_nt(lhs, kwin) + bias, k0

    def nat_finish(rr, s, k0, bounded):
        vwin = kb_buf[pl.ds(k0, NAT_HALO), B_W:2 * B_W]
        p = jnp.exp2(s) if bounded else jnp.exp2(s - jnp.max(s, axis=-1, keepdims=True))
        l = jnp.sum(p, axis=-1, keepdims=True)
        pv = _dot(p.astype(BF16), vwin)
        rows = [slice(GRID_W * h, GRID_W * (h + 1)) for h in range(B_HEADS)]
        o_buf[rr * GRID_W:(rr + 1) * GRID_W, A_Q:A_Q + B_W] = (
            _by_head([pv[r] for r in rows]) / _by_head([l[r] for r in rows]))
        return l

    def neighbourhood_and_memory(bounded):
        pending = nat_scores(0)
        l_min = None
        for rr in range(1, rows_per_tile + 1):
            ahead = nat_scores(rr) if rr < rows_per_tile else None
            l = nat_finish(rr - 1, *pending, bounded)
            l_min = l if l_min is None else jnp.minimum(l_min, l)
            pending = ahead

        km = kvm_ref[:, 0:M_W]
        vm = kvm_ref[:, M_W:2 * M_W]
        qm = q_ref[:, A_Q + B_W:Q_W]
        s = _dot_nt(qm, jnp.concatenate([km * head_b[h] for h in range(M_HEADS)], axis=0))
        probs, sums = [], []
        for h in range(M_HEADS):
            s_h = s[:, N_MEM * h:N_MEM * (h + 1)]
            shift = sink_ref[MEM_BOUND_SLOT] if bounded else jnp.max(s_h, axis=-1, keepdims=True)
            p = jnp.exp2(s_h - shift)
            sums.append(jnp.sum(p, axis=-1, keepdims=True))
            probs.append(p.astype(BF16))
        om = _dot(jnp.concatenate(probs, axis=1),
                  jnp.concatenate([vm * head_b[h] for h in range(M_HEADS)], axis=0))
        o_buf[:, A_Q + B_W:Q_W] = om / _by_head(sums)
        m_min = jnp.minimum(jnp.minimum(sums[0], sums[1]), jnp.minimum(sums[2], sums[3]))
        return jnp.minimum(jnp.min(l_min), jnp.min(m_min))

    smallest = neighbourhood_and_memory(True)

    @pl.when(smallest < MIN_DENOMINATOR)
    def _():
        neighbourhood_and_memory(False)

    o = jnp.concatenate(
        [_rms_scale(o_buf[:, 0:A_Q]), _rms_scale(o_buf[:, A_Q:A_Q + B_W]), _rms_scale(o_buf[:, A_Q + B_W:Q_W])],
        axis=-1) * og_ref[...]
    out_ref[...] = x_ref[...] + _dot(o.astype(BF16), wout_ref[...])


def _attention(x, qkv, kvm_all, wbias_tab, nat_all, layer, sink_l, og_l, wout_all):
    batch, seq, _ = x.shape
    tq = ATTN_TILE
    assert seq % tq == 0 and tq % NAT_HALO == 0 and seq >= 2 * A_BLOCK and seq >= NB_ROWS * GRID_W
    nt = seq // tq
    halo_per_tile = tq // NAT_HALO
    n_halo_blocks = seq // NAT_HALO
    const2 = lambda b, i: (0, 0)
    once = pl.Buffered(1)
    return pl.pallas_call(
        functools.partial(_attn_kernel, seq, tq),
        grid=(batch, nt),
        in_specs=[
            pl.BlockSpec((None, tq, D_MODEL), lambda b, i: (b, i, 0)),
            pl.BlockSpec((None, tq, PROJ_W), lambda b, i: (b, i, 0)),
            pl.BlockSpec((None, NAT_HALO, 2 * KV_W),
                         lambda b, i: (b, jnp.maximum(i * halo_per_tile - 1, 0), 1)),
            pl.BlockSpec((None, NAT_HALO, 2 * KV_W),
                         lambda b, i: (b, jnp.minimum((i + 1) * halo_per_tile, n_halo_blocks - 1), 1)),
            pl.BlockSpec((None, None, N_MEM, 2 * M_W), lambda b, i: (layer, b, 0, 0)),
            pl.BlockSpec((A_KV_HEADS, 2 * A_BLOCK, 6 * A_BLOCK), lambda b, i: (0, 0, 0), pipeline_mode=once),
            pl.BlockSpec((None, B_HEADS, 2 * NB_ROWS - 2, GRID_W, 2 * GRID_W),
                         lambda b, i: (layer, 0, 0, 0, 0), pipeline_mode=once),
            pl.BlockSpec(memory_space=pltpu.SMEM),
            pl.BlockSpec((1, Q_W), const2),
            pl.BlockSpec((None, Q_W, D_MODEL), lambda b, i: (layer, 0, 0), pipeline_mode=once),
        ],
        out_specs=pl.BlockSpec((None, tq, D_MODEL), lambda b, i: (b, i, 0)),
        out_shape=jax.ShapeDtypeStruct((batch, seq, D_MODEL), F32),
        scratch_shapes=[
            pltpu.VMEM((tq + 2 * A_BLOCK, 2 * 256), BF16),
            pltpu.VMEM((tq + 2 * A_BLOCK, 2 * 512), BF16),
            pltpu.VMEM((tq + 2 * NAT_HALO, KV_W), BF16),
        ],
        compiler_params=pltpu.CompilerParams(
            dimension_semantics=("arbitrary", "arbitrary"), vmem_limit_bytes=VMEM_LIMIT_BYTES),
        name="attention",
    )(x, qkv, qkv, qkv, kvm_all, wbias_tab, nat_all, sink_l, og_l, wout_all)


def _ffn_kernel(x_ref, g_ref, w1_ref, w2_ref, out_ref):
    x = x_ref[...]
    h = (_rms_scale(x) * g_ref[...]).astype(BF16)
    acc = x
    for c in range(D_FF // FF_CHUNK):
        f = _dot(h, w1_ref[:, c * FF_CHUNK:(c + 1) * FF_CHUNK])
        a = jnp.square(jnp.maximum(f, 0.0)).astype(BF16)
        acc = acc + _dot(a, w2_ref[c * FF_CHUNK:(c + 1) * FF_CHUNK, :])
    out_ref[...] = acc


def _ffn(x2d, g_ff_l, w1_all, w2_all, layer):
    tokens = x2d.shape[0]
    tm = FFN_TILE
    assert tokens % tm == 0
    const = lambda i: (0, 0)
    return pl.pallas_call(
        _ffn_kernel,
        grid=(tokens // tm,),
        in_specs=[
            pl.BlockSpec((tm, D_MODEL), lambda i: (i, 0)),
            pl.BlockSpec((1, D_MODEL), const),
            pl.BlockSpec((None, D_MODEL, D_FF), lambda i: (layer, 0, 0), pipeline_mode=pl.Buffered(1)),
            pl.BlockSpec((None, D_FF, D_MODEL), lambda i: (layer, 0, 0), pipeline_mode=pl.Buffered(1)),
        ],
        out_specs=pl.BlockSpec((tm, D_MODEL), lambda i: (i, 0)),
        out_shape=jax.ShapeDtypeStruct((tokens, D_MODEL), F32),
        compiler_params=pltpu.CompilerParams(
            dimension_semantics=("arbitrary",), vmem_limit_bytes=VMEM_LIMIT_BYTES),
        name="ffn",
    )(x2d, g_ff_l, w1_all, w2_all)


def _ffn_proj_kernel(x_ref, g_ref, w1_ref, w2_ref, gmix_ref, win_ref, gain_ref, ones_ref, out_ref, qkv_ref):
    ones = ones_ref[...]
    offs = (0, 256, 512, 768, KVA_OFF, KVB_OFF, KVA_OFF + 256, KVB_OFF + 256)
    sub = x_ref.shape[0] // PROJ_SUBTILES
    for t in range(PROJ_SUBTILES):
        rows = slice(t * sub, (t + 1) * sub)
        x = x_ref[rows, :]
        h = (_rms_scale(x) * g_ref[...]).astype(BF16)
        acc = x
        for c in range(D_FF // FF_CHUNK):
            f = _dot(h, w1_ref[:, c * FF_CHUNK:(c + 1) * FF_CHUNK])
            a = jnp.square(jnp.maximum(f, 0.0)).astype(BF16)
            acc = acc + _dot(a, w2_ref[c * FF_CHUNK:(c + 1) * FF_CHUNK, :])
        out_ref[rows, :] = acc
        h2 = (_rms_scale(acc) * gmix_ref[...]).astype(BF16)
        for pair in range(PROJ_W // (2 * LANE_TILE)):
            proj = _dot(h2, win_ref[:, 2 * pair * LANE_TILE:2 * (pair + 1) * LANE_TILE])
            for half in range(2):
                c = 2 * pair + half
                chunk = proj[:, half * LANE_TILE:(half + 1) * LANE_TILE]
                if c * LANE_TILE < NORMED_W:
                    chunk = _head_norm(chunk, ones, gain_ref[:, c * LANE_TILE:(c + 1) * LANE_TILE])
                qkv_ref[rows, offs[c]:offs[c] + LANE_TILE] = chunk.astype(BF16)


def _ffn_proj(x2d, g_ff_l, w1_all, w2_all, layer, g_mix_next, w_in_all, gain_next, ones):
    tokens = x2d.shape[0]
    tm = FFN_TILE
    assert tokens % tm == 0
    const = lambda i: (0, 0)
    once = pl.Buffered(1)
    return pl.pallas_call(
        _ffn_proj_kernel,
        grid=(tokens // tm,),
        in_specs=[
            pl.BlockSpec((tm, D_MODEL), lambda i: (i, 0)),
            pl.BlockSpec((1, D_MODEL), const),
            pl.BlockSpec((None, D_MODEL, D_FF), lambda i: (layer, 0, 0), pipeline_mode=once),
            pl.BlockSpec((None, D_FF, D_MODEL), lambda i: (layer, 0, 0), pipeline_mode=once),
            pl.BlockSpec((1, D_MODEL), const),
            pl.BlockSpec((None, D_MODEL, PROJ_W), lambda i: (layer + 1, 0, 0), pipeline_mode=once),
            pl.BlockSpec((1, NORMED_W), const),
            pl.BlockSpec((LANE_TILE, LANE_TILE), const),
        ],
        out_specs=[pl.BlockSpec((tm, D_MODEL), lambda i: (i, 0)), pl.BlockSpec((tm, PROJ_W), lambda i: (i, 0))],
        out_shape=[jax.ShapeDtypeStruct((tokens, D_MODEL), F32), jax.ShapeDtypeStruct((tokens, PROJ_W), BF16)],
        compiler_params=pltpu.CompilerParams(
            dimension_semantics=("arbitrary",), vmem_limit_bytes=VMEM_LIMIT_BYTES),
        name="ffn_proj",
    )(x2d, g_ff_l, w1_all, w2_all, g_mix_next, w_in_all, gain_next, ones)


def _window_bias_table():
    qi = np.arange(A_BLOCK)[:, None]
    si = np.arange(3 * A_BLOCK)[None, :]
    dist = np.abs(si - A_BLOCK - qi).astype(np.float32)
    slopes = (2.0 ** (-8.0 * np.arange(1, A_Q_HEADS + 1) / A_Q_HEADS)).astype(np.float32)
    tab = np.empty((A_KV_HEADS, 2 * A_BLOCK, 6 * A_BLOCK), np.float32)
    for g in range(A_KV_HEADS):
        for hh in range(4):
            rows = slice(A_BLOCK * (hh // 2), A_BLOCK * (hh // 2 + 1))
            cols = slice(3 * A_BLOCK * (hh % 2), 3 * A_BLOCK * (hh % 2 + 1))
            tab[g, rows, cols] = np.where(dist <= WINDOW, -(slopes[4 * g + hh] * dist) * LOG2E, -MASKED)
    return jnp.asarray(tab)


def _score_bound(gain_q, gain_k):
    return 1.02 * HEAD_DIM * (HEAD_DIM ** -0.5 * LOG2E) * jnp.max(jnp.abs(gain_q)) * jnp.max(jnp.abs(gain_k))


def _scalar_table(sink_l, qk_gain_l):
    g = qk_gain_l.astype(F32)
    return jnp.concatenate([sink_l.astype(F32) * LOG2E, _score_bound(g[4], g[5])[None],
                            jnp.zeros((16 - A_Q_HEADS - 1,), F32)])


def _neighbourhood_bias_table(rpb, qk_gain):
    c = np.arange(GRID_W)
    dc = np.clip(c[None, :] - c[:, None] + (NB_COLS - 1), 0, 2 * NB_COLS - 2)
    onehot = jnp.asarray((dc[None] == np.arange(2 * NB_COLS - 1)[:, None, None]).astype(np.float32))
    t = jnp.einsum("lhdc,cqk->lhdqk", rpb.astype(F32), onehot, precision=lax.Precision.HIGHEST) * LOG2E
    cstart = np.clip(c - NB_COLS // 2, 0, GRID_W - NB_COLS)
    cvalid = (c[None, :] >= cstart[:, None]) & (c[None, :] < cstart[:, None] + NB_COLS)
    g = qk_gain.astype(F32)
    qk_bound = jnp.stack([_score_bound(g[l, 2], g[l, 3]) for l in range(DEPTH)])
    bound = qk_bound[:, None] + jnp.max(rpb.astype(F32), axis=(2, 3)) * LOG2E
    t = jnp.where(cvalid, t - bound[:, :, None, None, None], -MASKED)
    return jnp.concatenate([t[:, :, :-1], t[:, :, 1:]], axis=-1)


def _pack_in_proj(w_in):
    qa, ka, va, qb, kb, vb, qm = jnp.split(
        w_in, [A_Q, A_Q + A_KV, A_Q + 2 * A_KV, A_Q + 2 * A_KV + B_W,
               A_Q + 2 * A_KV + 2 * B_W, A_Q + 2 * A_KV + 3 * B_W], axis=-1)
    twice = lambda t: jnp.concatenate(
        [t[..., :HEAD_DIM], t[..., :HEAD_DIM], t[..., HEAD_DIM:], t[..., HEAD_DIM:]], axis=-1)
    return jnp.concatenate([qa, qb, qm, twice(ka), kb, twice(va), vb], axis=-1).astype(BF16)


def _pack_qk_gain(qk_gain_l):
    scale = HEAD_DIM ** -0.5 * LOG2E
    g = qk_gain_l.astype(F32)
    return jnp.concatenate([
        jnp.tile(g[0], A_Q_HEADS) * scale, jnp.tile(g[2], B_HEADS) * scale, jnp.tile(g[4], M_HEADS) * scale,
        jnp.tile(g[1], 2 * A_KV_HEADS), jnp.tile(g[3], B_HEADS)])[None, :]


def _trunk(x, kvm, params):
    batch, seq, _ = x.shape
    p = params[0]
    qkv = _in_proj(x.reshape(batch * seq, D_MODEL), p["g_mix"], params["w_in"], 0, p["qk_gain"], params["ones"])
    for l in range(DEPTH):
        p = params[l]
        x = _attention(x, qkv.reshape(batch, seq, PROJ_W),
                       kvm, params["wbias"], params["nat"], l, p["sink"], p["o_gain"], params["w_out"])
        x2d = x.reshape(batch * seq, D_MODEL)
        if l + 1 < DEPTH:
            nxt = params[l + 1]
            x2d, qkv = _ffn_proj(x2d, p["g_ff"], params["w_ff1"], params["w_ff2"], l,
                                 nxt["g_mix"], params["w_in"], nxt["qk_gain"], params["ones"])
        else:
            x2d = _ffn(x2d, p["g_ff"], params["w_ff1"], params["w_ff2"], l)
        x = x2d.reshape(batch, seq, D_MODEL)
    return x


def kernel(x_prompt, x_sample, mem_prompt, mem_sample, g_mix, w_in, qk_gain, sink, rpb,
           o_gain, w_out, g_mem, w_mem_kv, g_ff, w_ff1, w_ff2):
    ones = jnp.asarray(
        np.kron(np.eye(LANE_TILE // HEAD_DIM), np.full((HEAD_DIM, HEAD_DIM), 1.0 / HEAD_DIM)), BF16)
    params = {
        "ones": ones, "wbias": _window_bias_table(), "nat": _neighbourhood_bias_table(rpb, qk_gain),
        "w_in": _pack_in_proj(w_in), "w_out": w_out.astype(BF16),
        "w_ff1": w_ff1.astype(BF16), "w_ff2": w_ff2.astype(BF16),
    }
    for l in range(DEPTH):
        params[l] = {
            "g_mix": g_mix[l].astype(F32)[None, :],
            "qk_gain": _pack_qk_gain(qk_gain[l]),
            "sink": _scalar_table(sink[l], qk_gain[l]),
            "o_gain": o_gain[l].astype(F32)[None, :],
            "g_ff": g_ff[l].astype(F32)[None, :],
        }
    km_gain = jnp.tile(qk_gain[:, 5].astype(F32), (1, M_HEADS))[:, None, :]
    g_mem3 = g_mem.astype(F32)[:, None, :]
    w_mem = w_mem_kv.astype(BF16)
    kvm_prompt = _mem_kv(mem_prompt, g_mem3, w_mem, km_gain, ones)
    kvm_sample = _mem_kv(mem_sample, g_mem3, w_mem, km_gain, ones)
    return (_trunk(x_prompt, kvm_prompt, params), _trunk(x_sample, kvm_sample, params))
```
